```python
import math
import jax, jax.numpy as jnp
from jax import lax
import numpy as np

D_MODEL = 4096
BATCH = 4
SEQ = 4096
DEPTH = 4

POOL_WINDOWS = (2, 4, 8, 16)
POOL_GROUP = D_MODEL // 8
POOL_WIDTH = POOL_GROUP * len(POOL_WINDOWS)
SSM_HEAD_DIM = 64
SSM_INNER = D_MODEL // 2
SSM_HEADS = SSM_INNER // SSM_HEAD_DIM
SSM_GROUPS = 4
SSM_STATE = 128
SSM_CONV = 4
SSM_CHUNK = 128
SSM_CONV_DIM = SSM_INNER + 2 * SSM_GROUPS * SSM_STATE
DT_MIN = 0.001
DT_MAX = 0.1
EVEN_IN = POOL_WIDTH + SSM_INNER + SSM_CONV_DIM + SSM_HEADS
EVEN_MIX = POOL_WIDTH + SSM_INNER
ATTN_HEAD_DIM = 128
ATTN_HEADS = D_MODEL // ATTN_HEAD_DIM
DILATED = ((128, 1), (512, 4), (2048, 16))
ATTN_BLOCK = 128
REL_BUCKETS = 32
REL_MAX_DIST = 2048
N_EXPERTS = 32
TOP_K = 4
D_EXPERT = 256
SWIGLU_LIMIT = 7.0
SWIGLU_ALPHA = 1.702
DEEPNORM_ALPHA = (2 * DEPTH) ** 0.25
DEEPNORM_BETA = (8 * DEPTH) ** -0.25
LN_EPS = 1e-5
RMS_EPS = 1e-5
N_EVEN = (DEPTH + 1) // 2
N_ODD = DEPTH // 2

kernel_name = "hybrid_pool_ssd_dilated_moe_deepnorm"


def layer_norm(x, g, b):
    xf = x.astype(jnp.float32)
    mu = jnp.mean(xf, -1, keepdims=True)
    var = jnp.mean(jnp.square(xf - mu), -1, keepdims=True)
    y = (xf - mu) * lax.rsqrt(var + LN_EPS)
    return (y * g.astype(jnp.float32) + b.astype(jnp.float32)).astype(x.dtype)


def multiscale_pool(u, pool_w, pool_scale):
    b, s, _ = u.shape
    uf = u.astype(jnp.float32)
    csum = jnp.cumsum(uf, axis=1)
    count = jnp.arange(1, s + 1, dtype=jnp.float32)[None, :, None]
    outs = []
    for g, w in enumerate(POOL_WINDOWS):
        sl = slice(g * POOL_GROUP, (g + 1) * POOL_GROUP)
        c = csum[..., sl]
        lag = jnp.pad(c, ((0, 0), (w, 0), (0, 0)))[:, :s]
        mean = (c - lag) / jnp.minimum(count, float(w))
        outs.append(mean - uf[..., sl])
    d = jnp.stack(outs, axis=2).astype(u.dtype)
    y = jnp.einsum('bsgc,gcd->bsgd', d, pool_w).reshape(b, s, POOL_WIDTH)
    return y * pool_scale


def causal_dwconv(u, w, bias):
    s = u.shape[1]
    k = w.shape[0]
    up = jnp.pad(u, ((0, 0), (k - 1, 0), (0, 0)))
    y = bias + up[:, 0:s] * w[0]
    for i in range(1, k):
        y = y + up[:, i:i + s] * w[i]
    return y


def ssd_scan(xh, dt, a, bm, cm):
    b, s, h, p = xh.shape
    g, n = bm.shape[2], bm.shape[3]
    r = h // g
    l = SSM_CHUNK
    nc = s // l
    xdt = (xh * dt[..., None]).reshape(b, nc, l, g, r, p)
    da = (dt * a).reshape(b, nc, l, g, r)
    bm = bm.reshape(b, nc, l, g, n)
    cm = cm.reshape(b, nc, l, g, n)
    acum = jnp.cumsum(da, axis=2)
    seg = acum[:, :, :, None] - acum[:, :, None, :]
    causal = np.tril(np.ones((l, l), dtype=bool))[None, None, :, :, None, None]
    decay = jnp.exp(jnp.where(causal, seg, -jnp.inf))
    cb = jnp.einsum('bctgn,bcsgn->bctsg', cm, bm)
    y_diag = jnp.einsum('bctsgr,bcsgrp->bctgrp', cb[..., None] * decay, xdt)
    decay_to_end = jnp.exp(acum[:, :, -1:] - acum)
    states = jnp.einsum('bcsgn,bcsgrp->bcgrpn', bm, decay_to_end[..., None] * xdt)
    chunk_decay = jnp.exp(acum[:, :, -1])

    def step(carry, inp):
        st, dec = inp
        return carry * dec[..., None, None] + st, carry

    init = jnp.zeros((b, g, r, p, n), xdt.dtype)
    _, prev = lax.scan(step, init, (jnp.moveaxis(states, 1, 0), jnp.moveaxis(chunk_decay, 1, 0)))
    prev = jnp.moveaxis(prev, 0, 1)
    y_off = jnp.einsum('bctgn,bcgrpn->bctgrp', cm, prev) * jnp.exp(acum)[..., None]
    return (y_diag + y_off).reshape(b, s, h, p)


def pool_ssd_mixer(x, w_in, pool_w, pool_scale, conv_w, conv_b, dt_bias, a_log, d_skip, norm_w, w_out):
    b, s, _ = x.shape
    f32 = jnp.float32
    u = x @ w_in
    pool_in, z, xbc, dt_raw = jnp.split(
        u, [POOL_WIDTH, POOL_WIDTH + SSM_INNER, POOL_WIDTH + SSM_INNER + SSM_CONV_DIM], axis=-1)
    y_pool = multiscale_pool(pool_in, pool_w, pool_scale)
    xbc = jax.nn.silu(causal_dwconv(xbc, conv_w, conv_b))
    xs, bm, cm = jnp.split(xbc, [SSM_INNER, SSM_INNER + SSM_GROUPS * SSM_STATE], axis=-1)
    xh = xs.reshape(b, s, SSM_HEADS, SSM_HEAD_DIM).astype(f32)
    dt = jax.nn.softplus(dt_raw.astype(f32) + dt_bias.astype(f32))
    a = -jnp.exp(a_log.astype(f32))
    y = ssd_scan(xh, dt, a,
                 bm.reshape(b, s, SSM_GROUPS, SSM_STATE).astype(f32),
                 cm.reshape(b, s, SSM_GROUPS, SSM_STATE).astype(f32))
    y = y + xh * d_skip.astype(f32)[:, None]
    y = y.reshape(b, s, SSM_INNER) * jax.nn.silu(z.astype(f32))
    yg = y.reshape(b, s, SSM_GROUPS, SSM_INNER // SSM_GROUPS)
    yg = yg * lax.rsqrt(jnp.mean(yg * yg, -1, keepdims=True) + RMS_EPS)
    y_ssd = (yg.reshape(b, s, SSM_INNER) * norm_w.astype(f32)).astype(x.dtype)
    return jnp.concatenate([y_pool, y_ssd], axis=-1) @ w_out


def t5_bucket(dist):
    max_exact = REL_BUCKETS // 2
    d = np.maximum(dist, 1).astype(np.float32)
    large = max_exact + (np.log(d / max_exact) / np.log(REL_MAX_DIST / max_exact)
                         * (REL_BUCKETS - max_exact)).astype(np.int32)
    large = np.minimum(large, REL_BUCKETS - 1)
    return np.where(dist < max_exact, dist, large).astype(np.int32)


def dilated_branch(q, k, v, rel_bias, dilation, n_back):
    b, s, h, e = q.shape
    f32 = jnp.float32
    blk = ATTN_BLOCK
    L = s // dilation
    nblk = -(-L // blk)
    Lp = nblk * blk

    def to_strided(t):
        t = t.reshape(b, L, dilation, h, e).transpose(0, 2, 1, 3, 4)
        t = jnp.pad(t, ((0, 0), (0, 0), (0, Lp - L), (0, 0), (0, 0)))
        return t.reshape(b, dilation, nblk, blk, h, e)

    def with_prev(t):
        prev = jnp.pad(t, ((0, 0), (0, 0), (1, 0), (0, 0), (0, 0), (0, 0)))[:, :, :nblk]
        return jnp.concatenate([prev, t], axis=3)

    qb = to_strided(q)
    kw = with_prev(to_strided(k))
    vw = with_prev(to_strided(v))
    scores = jnp.einsum('bdnqhe,bdnkhe->bdnhqk', qb, kw).astype(f32) * (e ** -0.5)
    qi = np.arange(blk)[:, None]
    kj = np.arange(2 * blk)[None, :]
    delta = blk + qi - kj
    band = (delta >= 0) & (delta <= n_back)
    rel = rel_bias[t5_bucket(np.clip(delta, 0, n_back) * dilation)]
    rel = jnp.transpose(rel, (2, 0, 1)).astype(f32)
    blk_idx = np.arange(nblk)[:, None, None]
    valid = band[None] & ((blk_idx > 0) | (kj[None] >= blk))
    scores = jnp.where(valid[:, None], scores + rel, -jnp.inf)
    m = jnp.max(scores, -1, keepdims=True)
    p = jnp.exp(scores - m)
    den = jnp.sum(p, -1, keepdims=True)
    o = jnp.einsum('bdnhqk,bdnkhe->bdnqhe', p, vw.astype(f32)) / jnp.swapaxes(den, 3, 4)
    lse = jnp.swapaxes((m + jnp.log(den))[..., 0], 3, 4)
    o = o.reshape(b, dilation, Lp, h, e)[:, :, :L].transpose(0, 2, 1, 3, 4).reshape(b, s, h, e)
    lse = lse.reshape(b, dilation, Lp, h)[:, :, :L].transpose(0, 2, 1, 3).reshape(b, s, h)
    return o, lse


def dilated_attention_mixer(x, w_qkv, w_out, rel_bias):
    b, s, _ = x.shape
    qkv = (x @ w_qkv).reshape(b, s, 3, ATTN_HEADS, ATTN_HEAD_DIM)
    q, k, v = qkv[:, :, 0], qkv[:, :, 1], qkv[:, :, 2]
    outs, lses = [], []
    for window, dil in DILATED:
        o, l = dilated_branch(q, k, v, rel_bias, dil, window // dil)
        outs.append(o)
        lses.append(l)
    wts = jax.nn.softmax(jnp.stack(lses, 0), axis=0)
    o = jnp.einsum('ibsh,ibshe->bshe', wts, jnp.stack(outs, 0))
    return o.reshape(b, s, D_MODEL).astype(x.dtype) @ w_out


def moe(x, router_w, router_b, w_gate_up, b_gate_up, w_down, b_down):
    b, s, d = x.shape
    f32 = jnp.float32
    xt = x.reshape(b * s, d)
    logits = (xt @ router_w + router_b).astype(f32)
    top_val, top_idx = lax.top_k(logits, TOP_K)
    gates = jax.nn.softmax(top_val, axis=-1)
    combine = jnp.einsum('tk,tke->te', gates, jax.nn.one_hot(top_idx, N_EXPERTS, dtype=f32))
    y = jnp.zeros((b * s, d), f32)
    for ex in range(N_EXPERTS):
        hdn = xt @ w_gate_up[ex] + b_gate_up[ex]
        gate = jnp.minimum(hdn[:, ::2], SWIGLU_LIMIT)
        up = jnp.clip(hdn[:, 1::2], -SWIGLU_LIMIT, SWIGLU_LIMIT)
        act = (up + 1.0) * gate * jax.nn.sigmoid(SWIGLU_ALPHA * gate)
        y = y + combine[:, ex:ex + 1] * (act @ w_down[ex] + b_down[ex]).astype(f32)
    return y.astype(x.dtype).reshape(b, s, d)


def setup_inputs(seed: int = 0) -> dict:
    key = jax.random.key(seed)
    ks = jax.random.split(key, 24)
    f32 = jnp.float32

    def nrm(k, shape, scale):
        return jax.random.normal(k, shape, f32) * scale

    dt0 = jnp.exp(jax.random.uniform(ks[7], (N_EVEN, SSM_HEADS), f32, math.log(DT_MIN), math.log(DT_MAX)))
    return {
        "x": nrm(ks[0], (BATCH, SEQ, D_MODEL), 1.0),
        "rel_bias": nrm(ks[1], (REL_BUCKETS, ATTN_HEADS), 0.5),
        "even_w_in": nrm(ks[2], (N_EVEN, D_MODEL, EVEN_IN), D_MODEL ** -0.5),
        "pool_w": nrm(ks[3], (N_EVEN, len(POOL_WINDOWS), POOL_GROUP, POOL_GROUP), POOL_GROUP ** -0.5),
        "pool_scale": 1.0 + nrm(ks[4], (N_EVEN, POOL_WIDTH), 0.1),
        "conv_w": nrm(ks[5], (N_EVEN, SSM_CONV, SSM_CONV_DIM), SSM_CONV ** -0.5),
        "conv_b": nrm(ks[6], (N_EVEN, SSM_CONV_DIM), 0.02),
        "dt_bias": dt0 + jnp.log(-jnp.expm1(-dt0)),
        "a_log": jnp.log(jax.random.uniform(ks[8], (N_EVEN, SSM_HEADS), f32, 1.0, 16.0)),
        "d_skip": 1.0 + nrm(ks[9], (N_EVEN, SSM_HEADS), 0.1),
        "ssm_norm_w": 1.0 + nrm(ks[10], (N_EVEN, SSM_INNER), 0.1),
        "even_w_out": nrm(ks[11], (N_EVEN, EVEN_MIX, D_MODEL), EVEN_MIX ** -0.5 * DEEPNORM_BETA),
        "attn_w_qkv": nrm(ks[12], (N_ODD, D_MODEL, 3 * D_MODEL), D_MODEL ** -0.5),
        "attn_w_out": nrm(ks[13], (N_ODD, D_MODEL, D_MODEL), D_MODEL ** -0.5 * DEEPNORM_BETA),
        "ln1_g": 1.0 + nrm(ks[14], (DEPTH, D_MODEL), 0.1),
        "ln1_b": nrm(ks[15], (DEPTH, D_MODEL), 0.02),
        "ln2_g": 1.0 + nrm(ks[16], (DEPTH, D_MODEL), 0.1),
        "ln2_b": nrm(ks[17], (DEPTH, D_MODEL), 0.02),
        "router_w": nrm(ks[18], (DEPTH, D_MODEL, N_EXPERTS), D_MODEL ** -0.5),
        "router_b": nrm(ks[19], (DEPTH, N_EXPERTS), 0.01),
        "w_gate_up": nrm(ks[20], (DEPTH, N_EXPERTS, D_MODEL, 2 * D_EXPERT), D_MODEL ** -0.5),
        "b_gate_up": nrm(ks[21], (DEPTH, N_EXPERTS, 2 * D_EXPERT), 0.02),
        "w_down": nrm(ks[22], (DEPTH, N_EXPERTS, D_EXPERT, D_MODEL), D_EXPERT ** -0.5 * DEEPNORM_BETA),
        "b_down": nrm(ks[23], (DEPTH, N_EXPERTS, D_MODEL), 0.02),
    }


def reference(x, rel_bias, even_w_in, pool_w, pool_scale, conv_w, conv_b, dt_bias, a_log, d_skip,
              ssm_norm_w, even_w_out, attn_w_qkv, attn_w_out, ln1_g, ln1_b, ln2_g, ln2_b,
              router_w, router_b, w_gate_up, b_gate_up, w_down, b_down):
    h = x
    for layer in range(DEPTH):
        i = layer // 2
        if layer % 2 == 0:
            mix = pool_ssd_mixer(h, even_w_in[i], pool_w[i], pool_scale[i], conv_w[i], conv_b[i],
                                 dt_bias[i], a_log[i], d_skip[i], ssm_norm_w[i], even_w_out[i])
        else:
            mix = dilated_attention_mixer(h, attn_w_qkv[i], attn_w_out[i], rel_bias)
        h = layer_norm(DEEPNORM_ALPHA * h + mix, ln1_g[layer], ln1_b[layer])
        ffn = moe(h, router_w[layer], router_b[layer], w_gate_up[layer], b_gate_up[layer],
                  w_down[layer], b_down[layer])
        h = layer_norm(DEEPNORM_ALPHA * h + ffn, ln2_g[layer], ln2_b[layer])
    return h
```

```python
import functools
import math

import numpy as np
import jax
import jax.numpy as jnp
from jax import lax
from jax.experimental import pallas as pl
from jax.experimental.pallas import tpu as pltpu

F32 = jnp.float32
BF16 = jnp.bfloat16

D_MODEL = 4096
DEPTH = 4
POOL_WINDOWS = (2, 4, 8, 16)
POOL_GROUP = D_MODEL // 8
POOL_WIDTH = POOL_GROUP * len(POOL_WINDOWS)
SSM_HEAD_DIM = 64
SSM_INNER = D_MODEL // 2
SSM_HEADS = SSM_INNER // SSM_HEAD_DIM
SSM_GROUPS = 4
SSM_STATE = 128
SSM_CONV = 4
SSM_CHUNK = 128
SSM_CONV_DIM = SSM_INNER + 2 * SSM_GROUPS * SSM_STATE
EVEN_MAIN = POOL_WIDTH + SSM_INNER + SSM_CONV_DIM
ATTN_HEAD_DIM = 128
ATTN_HEADS = D_MODEL // ATTN_HEAD_DIM
DILATED = ((128, 1), (512, 4), (2048, 16))
ATTN_BLOCK = 128
ATTN_SUPER = ATTN_BLOCK * 16
REL_BUCKETS = 32
REL_MAX_DIST = 2048
N_EXPERTS = 32
TOP_K = 4
D_EXPERT = 256
SWIGLU_LIMIT = 7.0
SWIGLU_ALPHA = 1.702
DEEPNORM_ALPHA = (2 * DEPTH) ** 0.25
LN_EPS = 1e-5
RMS_EPS = 1e-5

LANES = 128
NEG = -1e30
VMEM_LIMIT = 56 * 1024 * 1024
MOE_TILE = 256


def _cparams(n_axes):
    return pltpu.CompilerParams(dimension_semantics=("arbitrary",) * n_axes,
                                vmem_limit_bytes=VMEM_LIMIT)


def _mm_body(*refs, n_lhs, k_bounds, head_out):
    lhs = refs[:n_lhs]
    w_ref = refs[n_lhs]
    o_ref = refs[n_lhs + 1]
    acc = None
    for l_ref, (k0, k1) in zip(lhs, k_bounds):
        part = jnp.dot(l_ref[...], w_ref[k0:k1, :], preferred_element_type=F32)
        acc = part if acc is None else acc + part
    if head_out:
        for j in range(o_ref.shape[0]):
            o_ref[j] = acc[:, j * LANES:(j + 1) * LANES].astype(o_ref.dtype)
    else:
        o_ref[...] = acc.astype(o_ref.dtype)


def _matmul(lhs_list, w, n_out, tn, out_dtype, name, head_out=False):
    m = lhs_list[0].shape[0]
    k = w.shape[0]
    tm = min(1024, m)
    k_bounds = []
    k0 = 0
    for l in lhs_list:
        k_bounds.append((k0, k0 + l.shape[1]))
        k0 += l.shape[1]
    assert k0 == k and m % tm == 0 and n_out % tn == 0
    in_specs = [pl.BlockSpec((tm, l.shape[1]), lambda i, j: (i, 0)) for l in lhs_list]
    in_specs.append(pl.BlockSpec((k, tn), lambda i, j: (0, j)))
    if head_out:
        out_shape = jax.ShapeDtypeStruct((n_out // LANES, m, LANES), out_dtype)
        out_spec = pl.BlockSpec((tn // LANES, tm, LANES), lambda i, j: (j, i, 0))
    else:
        out_shape = jax.ShapeDtypeStruct((m, n_out), out_dtype)
        out_spec = pl.BlockSpec((tm, tn), lambda i, j: (i, j))
    return pl.pallas_call(
        functools.partial(_mm_body, n_lhs=len(lhs_list), k_bounds=tuple(k_bounds), head_out=head_out),
        out_shape=out_shape,
        grid=(m // tm, n_out // tn),
        in_specs=in_specs,
        out_specs=out_spec,
        compiler_params=_cparams(2),
        name=name,
    )(*lhs_list, w)


def _layer_norm_rows(x, g, b):
    mu = jnp.mean(x, axis=-1, keepdims=True)
    xc = x - mu
    var = jnp.mean(xc * xc, axis=-1, keepdims=True)
    return xc * lax.rsqrt(var + LN_EPS) * g + b


def _ln_body(h_ref, mix_ref, g_ref, b_ref, o_ref, obf_ref):
    y = _layer_norm_rows(DEEPNORM_ALPHA * h_ref[...] + mix_ref[...], g_ref[...], b_ref[...])
    o_ref[...] = y
    obf_ref[...] = y.astype(BF16)


def _residual_ln(h, mix, g, b):
    t, d = h.shape
    tm = min(256, t)
    row = pl.BlockSpec((tm, d), lambda i: (i, 0))
    vec = pl.BlockSpec((1, d), lambda i: (0, 0))
    return pl.pallas_call(
        _ln_body,
        out_shape=(jax.ShapeDtypeStruct((t, d), F32), jax.ShapeDtypeStruct((t, d), BF16)),
        grid=(t // tm,),
        in_specs=[row, row, vec, vec],
        out_specs=(row, row),
        compiler_params=_cparams(1),
        name="residual_ln",
    )(h, mix, g.reshape(1, d), b.reshape(1, d))


POOL_HALO = 16


def _pool_body(u_ref, halo_ref, w_ref, scale_ref, o_ref, buf, *, ts, tiles_per_seq):
    i = pl.program_id(0)
    first = (i % tiles_per_seq) == 0
    buf[0:POOL_HALO, :] = jnp.where(first, 0.0, halo_ref[...])
    buf[POOL_HALO:POOL_HALO + ts, :] = u_ref[...]
    pos = (i % tiles_per_seq) * ts + lax.broadcasted_iota(jnp.int32, (ts, 1), 0)
    for g, w in enumerate(POOL_WINDOWS):
        c0, c1 = g * POOL_GROUP, (g + 1) * POOL_GROUP
        cur = buf[POOL_HALO:POOL_HALO + ts, c0:c1]
        acc = cur
        for j in range(1, w):
            acc = acc + buf[POOL_HALO - j:POOL_HALO - j + ts, c0:c1]
        count = jnp.minimum(pos + 1, w).astype(F32)
        d = acc / count - cur
        y = jnp.dot(d.astype(BF16), w_ref[g], preferred_element_type=F32)
        o_ref[:, c0:c1] = (y * scale_ref[:, c0:c1]).astype(o_ref.dtype)


def _pool(u_main, pool_w_bf, pool_scale, seq):
    t = u_main.shape[0]
    ts = min(512, seq)
    tiles_per_seq = seq // ts
    hb = ts // POOL_HALO
    return pl.pallas_call(
        functools.partial(_pool_body, ts=ts, tiles_per_seq=tiles_per_seq),
        out_shape=jax.ShapeDtypeStruct((t, POOL_WIDTH), BF16),
        grid=(t // ts,),
        in_specs=[
            pl.BlockSpec((ts, POOL_WIDTH), lambda i: (i, 0)),
            pl.BlockSpec((POOL_HALO, POOL_WIDTH), lambda i: (jnp.maximum(i * hb - 1, 0), 0)),
            pl.BlockSpec((len(POOL_WINDOWS), POOL_GROUP, POOL_GROUP), lambda i: (0, 0, 0)),
            pl.BlockSpec((1, POOL_WIDTH), lambda i: (0, 0)),
        ],
        out_specs=pl.BlockSpec((ts, POOL_WIDTH), lambda i: (i, 0)),
        scratch_shapes=[pltpu.VMEM((ts + POOL_HALO, POOL_WIDTH), F32)],
        compiler_params=_cparams(1),
        name="pool_mixer",
    )(u_main, u_main, pool_w_bf, pool_scale.reshape(1, POOL_WIDTH))


CONV_HALO = 8
PAIR = 2 * SSM_HEAD_DIM
N_PAIRS = SSM_HEADS // 2
PAIRS_PER_GROUP = N_PAIRS // SSM_GROUPS


def _softplus(x):
    return jnp.maximum(x, 0.0) + jnp.log1p(jnp.exp(-jnp.abs(x)))


def _silu(x):
    return x * jax.nn.sigmoid(x)


def _ssd_body(z_ref, xs_ref, bc_ref, dt_ref, convw_ref, convb_ref, dtb_row_ref, dtb_col_ref,
              alog_row_ref, alog_col_ref, dskip_ref, normw_ref, o_ref,
              buf, xbc, carry, state, ybuf):
    L = SSM_CHUNK
    c = pl.program_id(1)

    @pl.when(c == 0)
    def _():
        carry[...] = jnp.zeros_like(carry)
        state[...] = jnp.zeros_like(state)

    buf[0:CONV_HALO, :] = carry[...]
    buf[CONV_HALO:CONV_HALO + L, 0:SSM_INNER] = xs_ref[...]
    buf[CONV_HALO:CONV_HALO + L, SSM_INNER:SSM_CONV_DIM] = bc_ref[...]
    carry[...] = buf[L:L + CONV_HALO, :]
    cw = 512
    for j in range(SSM_CONV_DIM // cw):
        cs = slice(j * cw, (j + 1) * cw)
        acc = convb_ref[:, cs] + buf[CONV_HALO - 3:CONV_HALO - 3 + L, cs] * convw_ref[0:1, cs]
        for i in range(1, SSM_CONV):
            off = CONV_HALO - 3 + i
            acc = acc + buf[off:off + L, cs] * convw_ref[i:i + 1, cs]
        xbc[:, cs] = _silu(acc)

    ri = lax.broadcasted_iota(jnp.int32, (L, L), 0)
    ci = lax.broadcasted_iota(jnp.int32, (L, L), 1)
    tril = ri >= ci
    dt_blk = dt_ref[...]
    dt = _softplus(dt_blk + dtb_row_ref[...])
    da = dt * (-jnp.exp(alog_row_ref[...]))
    acum = jnp.dot(tril.astype(F32), da, precision=lax.Precision.HIGHEST,
                   preferred_element_type=F32)
    dt_t = _softplus(dt_blk.T + dtb_col_ref[...])
    da_t = dt_t * (-jnp.exp(alog_col_ref[...]))
    acum_t = jnp.dot(da_t, (ri <= ci).astype(F32), precision=lax.Precision.HIGHEST,
                     preferred_element_type=F32)
    w_t = jnp.exp(acum_t[:, L - 1:L] - acum_t) * dt_t
    eac = jnp.exp(acum)
    cd_row = jnp.exp(acum[L - 1:L, :])

    lane = lax.broadcasted_iota(jnp.int32, (L, PAIR), 1)
    lane_row = lax.broadcasted_iota(jnp.int32, (1, PAIR), 1)
    nt = (((1,), (1,)), ((), ()))
    for g in range(SSM_GROUPS):
        b0 = SSM_INNER + g * SSM_STATE
        c0 = SSM_INNER + SSM_GROUPS * SSM_STATE + g * SSM_STATE
        bm = xbc[:, b0:b0 + SSM_STATE]
        cm = xbc[:, c0:c0 + SSM_STATE]
        cb = lax.dot_general(cm.astype(BF16), bm.astype(BF16), nt, preferred_element_type=F32)
        bm_t = bm.T
        for j in range(PAIRS_PER_GROUP):
            p = g * PAIRS_PER_GROUP + j
            xs_pair = xbc[:, p * PAIR:(p + 1) * PAIR].astype(BF16)
            st = state[p]
            lhs_y, lhs_s = [], []
            for h in (2 * p, 2 * p + 1):
                seg = acum[:, h:h + 1] - acum_t[h:h + 1, :]
                dec = jnp.exp(jnp.where(tril, seg, -jnp.inf))
                m_h = cb * dec * dt_t[h:h + 1, :]
                e_h = eac[:, h:h + 1] * cm
                lhs_y.append(jnp.concatenate([m_h, e_h], axis=1).astype(BF16))
                lhs_s.append((bm_t * w_t[h:h + 1, :]).astype(BF16))
            rhs = jnp.concatenate([xs_pair, st.astype(BF16)], axis=0)
            r = jnp.dot(jnp.concatenate(lhs_y, axis=0), rhs, preferred_element_type=F32)
            ybuf[:, p * PAIR:(p + 1) * PAIR] = jnp.where(lane < SSM_HEAD_DIM, r[0:L], r[L:2 * L])
            s_new = jnp.dot(jnp.concatenate(lhs_s, axis=0), xs_pair, preferred_element_type=F32)
            contrib = jnp.where(lane < SSM_HEAD_DIM, s_new[0:L], s_new[L:2 * L])
            cdp = jnp.where(lane_row < SSM_HEAD_DIM, cd_row[:, 2 * p:2 * p + 1], cd_row[:, 2 * p + 1:2 * p + 2])
            state[p] = st * cdp + contrib

    gw = SSM_INNER // SSM_GROUPS
    for g in range(SSM_GROUPS):
        cs = slice(g * gw, (g + 1) * gw)
        y = ybuf[:, cs] + xbc[:, cs] * dskip_ref[:, cs]
        y = y * _silu(z_ref[:, cs])
        ms = jnp.mean(y * y, axis=-1, keepdims=True)
        o_ref[:, cs] = (y * lax.rsqrt(ms + RMS_EPS) * normw_ref[:, cs]).astype(o_ref.dtype)


def _ssd(u_main, dt_raw, conv_w, conv_b, dt_bias, a_log, d_skip, norm_w, batch, seq):
    t = u_main.shape[0]
    L = SSM_CHUNK
    nc = seq // L
    pad = LANES - SSM_HEADS
    dtb = jnp.pad(dt_bias.astype(F32), (0, pad))
    alog = jnp.pad(a_log.astype(F32), (0, pad))
    dskip = jnp.repeat(d_skip.astype(F32), SSM_HEAD_DIM).reshape(1, SSM_INNER)
    zcol = POOL_WIDTH // SSM_INNER
    xcol = (POOL_WIDTH + SSM_INNER) // SSM_INNER
    bc_w = 2 * SSM_GROUPS * SSM_STATE
    bcol = (POOL_WIDTH + 2 * SSM_INNER) // bc_w
    rowmap = lambda col: (lambda b, c: (b * nc + c, col))
    full = lambda shape: pl.BlockSpec(shape, lambda b, c: (0,) * len(shape))
    return pl.pallas_call(
        _ssd_body,
        out_shape=jax.ShapeDtypeStruct((t, SSM_INNER), BF16),
        grid=(batch, nc),
        in_specs=[
            pl.BlockSpec((L, SSM_INNER), rowmap(zcol)),
            pl.BlockSpec((L, SSM_INNER), rowmap(xcol)),
            pl.BlockSpec((L, bc_w), rowmap(bcol)),
            pl.BlockSpec((L, LANES), rowmap(0)),
            full((SSM_CONV, SSM_CONV_DIM)),
            full((1, SSM_CONV_DIM)),
            full((1, LANES)),
            full((LANES, 1)),
            full((1, LANES)),
            full((LANES, 1)),
            full((1, SSM_INNER)),
            full((1, SSM_INNER)),
        ],
        out_specs=pl.BlockSpec((L, SSM_INNER), rowmap(0)),
        scratch_shapes=[
            pltpu.VMEM((L + CONV_HALO, SSM_CONV_DIM), F32),
            pltpu.VMEM((L, SSM_CONV_DIM), F32),
            pltpu.VMEM((CONV_HALO, SSM_CONV_DIM), F32),
            pltpu.VMEM((N_PAIRS, SSM_STATE, PAIR), F32),
            pltpu.VMEM((L, SSM_INNER), F32),
        ],
        compiler_params=_cparams(2),
        name="ssd_mixer",
    )(u_main, u_main, u_main, dt_raw, conv_w.astype(F32), conv_b.reshape(1, SSM_CONV_DIM).astype(F32),
      dtb.reshape(1, LANES), dtb.reshape(LANES, 1), alog.reshape(1, LANES), alog.reshape(LANES, 1),
      dskip, norm_w.reshape(1, SSM_INNER).astype(F32))


def _t5_bucket(dist):
    max_exact = REL_BUCKETS // 2
    d = np.maximum(dist, 1).astype(np.float32)
    large = max_exact + (np.log(d / max_exact) / np.log(REL_MAX_DIST / max_exact)
                         * (REL_BUCKETS - max_exact)).astype(np.int32)
    large = np.minimum(large, REL_BUCKETS - 1)
    return np.where(dist < max_exact, dist, large).astype(np.int32)


def _attn_bias_tables(rel_bias):
    blk = ATTN_BLOCK
    delta = blk + np.arange(blk)[:, None] - np.arange(2 * blk)[None, :]
    tabs = []
    for window, dil in DILATED:
        n_back = window // dil
        assert n_back <= blk
        band = (delta >= 0) & (delta <= n_back)
        rel = rel_bias[_t5_bucket(np.clip(delta, 0, n_back) * dil)]
        rel = jnp.transpose(rel, (2, 0, 1)).astype(F32)
        tabs.append(jnp.where(band[None], rel, NEG))
    return jnp.stack(tabs, 0)


def _attn_body(q_ref, kp_ref, kc_ref, vp_ref, vc_ref, bias_ref, o_ref, m_ref, l_ref, acc_ref):
    blk = ATTN_BLOCK
    seq_first = pl.program_id(2) == 0
    scale = ATTN_HEAD_DIM ** -0.5
    nt = (((1,), (1,)), ((), ()))
    m_ref[...] = jnp.full_like(m_ref, NEG)
    l_ref[...] = jnp.zeros_like(l_ref)
    acc_ref[...] = jnp.zeros_like(acc_ref)

    def rows(start, d):
        return pl.ds(start, blk) if d == 1 else pl.ds(start, blk, stride=d)

    for bi, (_, d) in enumerate(DILATED):
        nblk = ATTN_SUPER // (d * blk)
        bias_prev = bias_ref[bi, :, 0:blk]
        bias_cur = bias_ref[bi, :, blk:2 * blk]
        bias_prev_first = jnp.where(seq_first, NEG, bias_prev)

        def block(qs, kp_src, vp_src, kps, bias0, d=d, bias_cur=bias_cur):
            qr = rows(qs, d)
            pr = rows(kps, d)
            q = (q_ref[qr, :] * scale).astype(BF16)
            s0 = lax.dot_general(q, kp_src[pr, :].astype(BF16), nt, preferred_element_type=F32) + bias0
            s1 = lax.dot_general(q, kc_ref[qr, :].astype(BF16), nt, preferred_element_type=F32) + bias_cur
            m_prev = m_ref[qr, :]
            rm = jnp.maximum(jnp.max(s0, axis=1, keepdims=True), jnp.max(s1, axis=1, keepdims=True))
            m_new = jnp.maximum(m_prev, rm)
            alpha = jnp.exp(m_prev - m_new)
            p0 = jnp.exp(s0 - m_new)
            p1 = jnp.exp(s1 - m_new)
            rs = jnp.sum(p0, axis=1, keepdims=True) + jnp.sum(p1, axis=1, keepdims=True)
            pv = (jnp.dot(p0.astype(BF16), vp_src[pr, :].astype(BF16), preferred_element_type=F32)
                  + jnp.dot(p1.astype(BF16), vc_ref[qr, :].astype(BF16), preferred_element_type=F32))
            l_ref[qr, :] = alpha * l_ref[qr, :] + rs
            acc_ref[qr, :] = alpha * acc_ref[qr, :] + pv
            m_ref[qr, :] = m_new

        def residue(r, carry, d=d, nblk=nblk, block=block, bias_prev=bias_prev,
                    bias_prev_first=bias_prev_first):
            block(r, kp_ref, vp_ref, ATTN_SUPER - blk * d + r, bias_prev_first)

            def inner(n, c2):
                qs = n * (blk * d) + r
                block(qs, kc_ref, vc_ref, qs - blk * d, bias_prev)
                return c2

            if nblk > 1:
                lax.fori_loop(1, nblk, inner, 0)
            return carry

        if d == 1:
            residue(0, 0)
        else:
            lax.fori_loop(0, d, residue, 0)

    o_ref[...] = (acc_ref[...] / l_ref[...]).astype(o_ref.dtype)


def _attention(qkv_heads, bias_tabs, batch, seq):
    t = qkv_heads.shape[1]
    hn = ATTN_HEADS
    sup = ATTN_SUPER
    nst = seq // sup
    assert seq % sup == 0
    cur = lambda off: (lambda b, h, s: (off + h, b * nst + s, 0))
    prev = lambda off: (lambda b, h, s: (off + h, b * nst + jnp.maximum(s - 1, 0), 0))
    slab = lambda imap: pl.BlockSpec((None, sup, LANES), imap)
    return pl.pallas_call(
        _attn_body,
        out_shape=jax.ShapeDtypeStruct((t, D_MODEL), BF16),
        grid=(batch, hn, nst),
        in_specs=[
            slab(cur(0)),
            slab(prev(hn)), slab(cur(hn)),
            slab(prev(2 * hn)), slab(cur(2 * hn)),
            pl.BlockSpec((len(DILATED), None, ATTN_BLOCK, 2 * ATTN_BLOCK), lambda b, h, s: (0, h, 0, 0)),
        ],
        out_specs=pl.BlockSpec((sup, LANES), lambda b, h, s: (b * nst + s, h)),
        scratch_shapes=[pltpu.VMEM((sup, LANES), F32)] * 3,
        compiler_params=_cparams(3),
        name="dilated_attention",
    )(qkv_heads, qkv_heads, qkv_heads, qkv_heads, qkv_heads, bias_tabs)


def _router_body(h_ref, w_ref, b_ref, idx_ref, gate_ref):
    logits = jnp.dot(h_ref[...], w_ref[...], preferred_element_type=F32) + b_ref[...]
    lane = lax.broadcasted_iota(jnp.int32, logits.shape, 1)
    vals, idxs = [], []
    for _ in range(TOP_K):
        m = jnp.max(logits, axis=1, keepdims=True)
        idx = jnp.min(jnp.where(logits == m, lane, LANES), axis=1, keepdims=True)
        vals.append(m)
        idxs.append(idx)
        logits = jnp.where(lane == idx, -jnp.inf, logits)
    es = [jnp.exp(v - vals[0]) for v in vals]
    den = es[0]
    for e in es[1:]:
        den = den + e
    idx_out = jnp.zeros(logits.shape, jnp.int32)
    gate_out = jnp.zeros(logits.shape, F32)
    for k in range(TOP_K):
        idx_out = jnp.where(lane == k, idxs[k], idx_out)
        gate_out = jnp.where(lane == k, es[k] / den, gate_out)
    idx_ref[...] = idx_out
    gate_ref[...] = gate_out


def _router(h_bf, router_w, router_b):
    t, d = h_bf.shape
    tm = min(1024, t)
    pad = LANES - N_EXPERTS
    w = jnp.pad(router_w, ((0, 0), (0, pad))).astype(BF16)
    b = jnp.pad(router_b.astype(F32), (0, pad), constant_values=NEG).reshape(1, LANES)
    idx, gates = pl.pallas_call(
        _router_body,
        out_shape=(jax.ShapeDtypeStruct((t, LANES), jnp.int32), jax.ShapeDtypeStruct((t, LANES), F32)),
        grid=(t // tm,),
        in_specs=[pl.BlockSpec((tm, d), lambda i: (i, 0)),
                  pl.BlockSpec((d, LANES), lambda i: (0, 0)),
                  pl.BlockSpec((1, LANES), lambda i: (0, 0))],
        out_specs=(pl.BlockSpec((tm, LANES), lambda i: (i, 0)), pl.BlockSpec((tm, LANES), lambda i: (i, 0))),
        compiler_params=_cparams(1),
        name="moe_router",
    )(h_bf, w, b)
    return idx[:, :TOP_K], gates[:, :TOP_K]


def _route_plan(top_idx, gates, tm):
    t = top_idx.shape[0]
    a = t * TOP_K
    e = top_idx.reshape(a)
    onehot = (e[:, None] == jnp.arange(N_EXPERTS, dtype=jnp.int32)[None, :]).astype(jnp.int32)
    csum = jnp.cumsum(onehot, axis=0)
    rank = jnp.sum(csum * onehot, axis=1) - 1
    counts = csum[-1]
    padded = ((counts + tm - 1) // tm) * tm
    ends = jnp.cumsum(padded)
    starts = ends - padded
    slot = (starts[e] + rank).astype(jnp.int32)
    p = a + N_EXPERTS * tm
    token = (jnp.arange(a, dtype=jnp.int32) // TOP_K)
    row_src = jnp.zeros((p,), jnp.int32).at[slot].set(token, unique_indices=True)
    gate_sorted = jnp.zeros((p,), F32).at[slot].set(gates.reshape(a), unique_indices=True)
    tile_start = jnp.arange(p // tm, dtype=jnp.int32) * tm
    tile_expert = jnp.minimum(jnp.searchsorted(ends, tile_start, side="right"), N_EXPERTS - 1).astype(jnp.int32)
    n_used = (ends[-1] // tm).astype(jnp.int32).reshape(1)
    return slot, row_src, gate_sorted, tile_expert, n_used


def _expert_body(te_ref, nused_ref, src_ref, h_ref, gate_ref, wgu_ref, bgu_ref, wd_ref, bd_ref, o_ref,
                 xbuf, sem, *, tm):
    i = pl.program_id(0)
    used = i < nused_ref[0]

    @pl.when(used)
    def _():
        base = i * tm

        def issue(r, c):
            tok = src_ref[base + r]
            pltpu.make_async_copy(h_ref.at[pl.ds(tok, 1)], xbuf.at[pl.ds(r, 1)], sem).start()
            return c

        lax.fori_loop(0, tm, issue, 0, unroll=8)
        pltpu.make_async_copy(h_ref.at[pl.ds(0, tm)], xbuf, sem).wait()
        x = xbuf[...].astype(BF16)
        hdn = jnp.dot(x, wgu_ref[...], preferred_element_type=F32) + bgu_ref[...]
        gate = jnp.minimum(hdn[:, :D_EXPERT], SWIGLU_LIMIT)
        up = jnp.clip(hdn[:, D_EXPERT:], -SWIGLU_LIMIT, SWIGLU_LIMIT)
        act = (up + 1.0) * gate * jax.nn.sigmoid(SWIGLU_ALPHA * gate)
        y = jnp.dot(act.astype(BF16), wd_ref[...], preferred_element_type=F32) + bd_ref[...]
        o_ref[...] = y * gate_ref[...]

    @pl.when(jnp.logical_not(used))
    def _():
        o_ref[...] = jnp.zeros_like(o_ref)


def _experts(h, row_src, gate_sorted, tile_expert, n_used, wgu, bgu, wd, bd, tm):
    p = row_src.shape[0]
    d = h.shape[1]
    return pl.pallas_call(
        functools.partial(_expert_body, tm=tm),
        out_shape=jax.ShapeDtypeStruct((p, d), F32),
        grid_spec=pltpu.PrefetchScalarGridSpec(
            num_scalar_prefetch=3,
            grid=(p // tm,),
            in_specs=[
                pl.BlockSpec(memory_space=pl.ANY),
                pl.BlockSpec((tm, 1), lambda i, te, nu, src: (i, 0)),
                pl.BlockSpec((None, d, 2 * D_EXPERT), lambda i, te, nu, src: (te[i], 0, 0)),
                pl.BlockSpec((None, 1, 2 * D_EXPERT), lambda i, te, nu, src: (te[i], 0, 0)),
                pl.BlockSpec((None, D_EXPERT, d), lambda i, te, nu, src: (te[i], 0, 0)),
                pl.BlockSpec((None, 1, d), lambda i, te, nu, src: (te[i], 0, 0)),
            ],
            out_specs=pl.BlockSpec((tm, d), lambda i, te, nu, src: (i, 0)),
            scratch_shapes=[pltpu.VMEM((tm, d), F32), pltpu.SemaphoreType.DMA],
        ),
        compiler_params=_cparams(1),
        name="moe_experts",
    )(tile_expert, n_used, row_src, h, gate_sorted.reshape(p, 1), wgu, bgu, wd, bd)


def _combine_body(slot_ref, y_ref, h_ref, g_ref, b_ref, o_ref, obf_ref, ybuf, sem, *, tc):
    i = pl.program_id(0)
    base = i * tc

    def issue(r, c):
        for k in range(TOP_K):
            s = slot_ref[(base + r) * TOP_K + k]
            pltpu.make_async_copy(y_ref.at[pl.ds(s, 1)], ybuf.at[k, pl.ds(r, 1)], sem).start()
        return c

    lax.fori_loop(0, tc, issue, 0, unroll=4)
    for k in range(TOP_K):
        pltpu.make_async_copy(y_ref.at[pl.ds(0, tc)], ybuf.at[k], sem).wait()
    ffn = ybuf[0]
    for k in range(1, TOP_K):
        ffn = ffn + ybuf[k]
    y = _layer_norm_rows(DEEPNORM_ALPHA * h_ref[...] + ffn, g_ref[...], b_ref[...])
    o_ref[...] = y
    obf_ref[...] = y.astype(BF16)


def _combine_ln(slot, y_sorted, h, g, b):
    t, d = h.shape
    tc = min(128, t)
    row = lambda i, s: (i, 0)
    vec = lambda i, s: (0, 0)
    return pl.pallas_call(
        functools.partial(_combine_body, tc=tc),
        out_shape=(jax.ShapeDtypeStruct((t, d), F32), jax.ShapeDtypeStruct((t, d), BF16)),
        grid_spec=pltpu.PrefetchScalarGridSpec(
            num_scalar_prefetch=1,
            grid=(t // tc,),
            in_specs=[
                pl.BlockSpec(memory_space=pl.ANY),
                pl.BlockSpec((tc, d), row),
                pl.BlockSpec((1, d), vec),
                pl.BlockSpec((1, d), vec),
            ],
            out_specs=(pl.BlockSpec((tc, d), row), pl.BlockSpec((tc, d), row)),
            scratch_shapes=[pltpu.VMEM((TOP_K, tc, d), F32), pltpu.SemaphoreType.DMA],
        ),
        compiler_params=_cparams(1),
        name="moe_combine_ln",
    )(slot, y_sorted, h, g.reshape(1, d), b.reshape(1, d))


def _moe_ln(h, h_bf, router_w, router_b, w_gate_up, b_gate_up, w_down, b_down, g, b):
    tm = MOE_TILE
    top_idx, gates = _router(h_bf, router_w, router_b)
    slot, row_src, gate_sorted, tile_expert, n_used = _route_plan(top_idx, gates, tm)
    wgu = jnp.concatenate([w_gate_up[..., 0::2], w_gate_up[..., 1::2]], axis=-1).astype(BF16)
    bgu = jnp.concatenate([b_gate_up[..., 0::2], b_gate_up[..., 1::2]], axis=-1).astype(F32)
    bgu = bgu.reshape(N_EXPERTS, 1, 2 * D_EXPERT)
    wd = w_down.astype(BF16)
    bd = b_down.astype(F32).reshape(N_EXPERTS, 1, D_MODEL)
    y_sorted = _experts(h, row_src, gate_sorted, tile_expert, n_used, wgu, bgu, wd, bd, tm)
    return _combine_ln(slot, y_sorted, h, g, b)


def _even_mixer(h_bf, w_in, pool_w, pool_scale, conv_w, conv_b, dt_bias, a_log, d_skip, norm_w, w_out,
                batch, seq):
    w_in_bf = w_in.astype(BF16)
    w_dt = jnp.pad(w_in[:, EVEN_MAIN:], ((0, 0), (0, LANES - SSM_HEADS))).astype(BF16)
    u_main = _matmul([h_bf], w_in_bf, EVEN_MAIN, 512, F32, "even_in_proj")
    dt_raw = _matmul([h_bf], w_dt, LANES, LANES, F32, "even_dt_proj")
    y_pool = _pool(u_main, pool_w.astype(BF16), pool_scale.astype(F32), seq)
    y_ssd = _ssd(u_main, dt_raw, conv_w, conv_b, dt_bias, a_log, d_skip, norm_w, batch, seq)
    return _matmul([y_pool, y_ssd], w_out.astype(BF16), D_MODEL, 512, F32, "even_out_proj")


def _attn_mixer(h_bf, w_qkv, w_out, bias_tabs, batch, seq):
    qkv_heads = _matmul([h_bf], w_qkv.astype(BF16), 3 * D_MODEL, 512, F32, "attn_qkv_proj", head_out=True)
    o = _attention(qkv_heads, bias_tabs, batch, seq)
    return _matmul([o], w_out.astype(BF16), D_MODEL, 512, F32, "attn_out_proj")


def kernel(x, rel_bias, even_w_in, pool_w, pool_scale, conv_w, conv_b, dt_bias, a_log, d_skip, ssm_norm_w, even_w_out, attn_w_qkv, attn_w_out, ln1_g, ln1_b, ln2_g, ln2_b, router_w, router_b, w_gate_up, b_gate_up, w_down, b_down):
    batch, seq, d = x.shape
    h = x.reshape(batch * seq, d).astype(F32)
    h_bf = h.astype(BF16)
    bias_tabs = _attn_bias_tables(rel_bias)
    for layer in range(DEPTH):
        i = layer // 2
        if layer % 2 == 0:
            mix = _even_mixer(h_bf, even_w_in[i], pool_w[i], pool_scale[i], conv_w[i], conv_b[i],
                              dt_bias[i], a_log[i], d_skip[i], ssm_norm_w[i], even_w_out[i], batch, seq)
        else:
            mix = _attn_mixer(h_bf, attn_w_qkv[i], attn_w_out[i], bias_tabs, batch, seq)
        h, h_bf = _residual_ln(h, mix, ln1_g[layer], ln1_b[layer])
        h, h_bf = _moe_ln(h, h_bf, router_w[layer], router_b[layer], w_gate_up[layer], b_gate_up[layer],
                          w_down[layer], b_down[layer], ln2_g[layer], ln2_b[layer])
    return h.reshape(batch, seq, d).astype(x.dtype)
```

```python
import functools
import math

import numpy as np
import jax
import jax.numpy as jnp
from jax import lax
from jax.experimental import pallas as pl
from jax.experimental.pallas import tpu as pltpu

F32 = jnp.float32
BF16 = jnp.bfloat16

D_MODEL = 4096
DEPTH = 4
POOL_WINDOWS = (2, 4, 8, 16)
POOL_GROUP = D_MODEL // 8
POOL_WIDTH = POOL_GROUP * len(POOL_WINDOWS)
SSM_HEAD_DIM = 64
SSM_INNER = D_MODEL // 2
SSM_HEADS = SSM_INNER // SSM_HEAD_DIM
SSM_GROUPS = 4
SSM_STATE = 128
SSM_CONV = 4
SSM_CHUNK = 128
SSM_CONV_DIM = SSM_INNER + 2 * SSM_GROUPS * SSM_STATE
EVEN_MAIN = POOL_WIDTH + SSM_INNER + SSM_CONV_DIM
ATTN_HEAD_DIM = 128
ATTN_HEADS = D_MODEL // ATTN_HEAD_DIM
DILATED = ((128, 1), (512, 4), (2048, 16))
ATTN_BLOCK = 128
ATTN_SUPER = ATTN_BLOCK * 16
REL_BUCKETS = 32
REL_MAX_DIST = 2048
N_EXPERTS = 32
TOP_K = 4
D_EXPERT = 256
SWIGLU_LIMIT = 7.0
SWIGLU_ALPHA = 1.702
DEEPNORM_ALPHA = (2 * DEPTH) ** 0.25
LN_EPS = 1e-5
RMS_EPS = 1e-5

LANES = 128
NEG = -1e30
VMEM_LIMIT = 56 * 1024 * 1024
MOE_TILE = 256


def _cparams(n_axes):
    return pltpu.CompilerParams(dimension_semantics=("arbitrary",) * n_axes,
                                vmem_limit_bytes=VMEM_LIMIT)


def _mm_body(*refs, n_lhs, k_bounds, head_out, scaled_blocks, scale):
    lhs = refs[:n_lhs]
    w_ref = refs[n_lhs]
    o_ref = refs[n_lhs + 1]
    acc = None
    for l_ref, (k0, k1) in zip(lhs, k_bounds):
        part = jnp.dot(l_ref[...], w_ref[k0:k1, :], preferred_element_type=F32)
        acc = part if acc is None else acc + part
    if scaled_blocks:
        acc = acc * jnp.where(pl.program_id(1) < scaled_blocks, scale, 1.0)
    if head_out:
        for j in range(o_ref.shape[0]):
            o_ref[j] = acc[:, j * LANES:(j + 1) * LANES].astype(o_ref.dtype)
    else:
        o_ref[...] = acc.astype(o_ref.dtype)


def _matmul(lhs_list, w, n_out, tn, out_dtype, name, head_out=False, scaled_cols=0, scale=1.0):
    assert scaled_cols % tn == 0
    m = lhs_list[0].shape[0]
    k = w.shape[0]
    tm = min(1024, m)
    k_bounds = []
    k0 = 0
    for l in lhs_list:
        k_bounds.append((k0, k0 + l.shape[1]))
        k0 += l.shape[1]
    assert k0 == k and m % tm == 0 and n_out % tn == 0
    in_specs = [pl.BlockSpec((tm, l.shape[1]), lambda i, j: (i, 0)) for l in lhs_list]
    in_specs.append(pl.BlockSpec((k, tn), lambda i, j: (0, j)))
    if head_out:
        out_shape = jax.ShapeDtypeStruct((n_out // LANES, m, LANES), out_dtype)
        out_spec = pl.BlockSpec((tn // LANES, tm, LANES), lambda i, j: (j, i, 0))
    else:
        out_shape = jax.ShapeDtypeStruct((m, n_out), out_dtype)
        out_spec = pl.BlockSpec((tm, tn), lambda i, j: (i, j))
    return pl.pallas_call(
        functools.partial(_mm_body, n_lhs=len(lhs_list), k_bounds=tuple(k_bounds), head_out=head_out,
                          scaled_blocks=scaled_cols // tn, scale=scale),
        out_shape=out_shape,
        grid=(m // tm, n_out // tn),
        in_specs=in_specs,
        out_specs=out_spec,
        compiler_params=_cparams(2),
        name=name,
    )(*lhs_list, w)


def _layer_norm_rows(x, g, b):
    mu = jnp.mean(x, axis=-1, keepdims=True)
    xc = x - mu
    var = jnp.mean(xc * xc, axis=-1, keepdims=True)
    return xc * lax.rsqrt(var + LN_EPS) * g + b


def _ln_body(h_ref, mix_ref, g_ref, b_ref, o_ref, obf_ref):
    y = _layer_norm_rows(DEEPNORM_ALPHA * h_ref[...] + mix_ref[...], g_ref[...], b_ref[...])
    o_ref[...] = y
    obf_ref[...] = y.astype(BF16)


def _residual_ln(h, mix, g, b):
    t, d = h.shape
    tm = min(256, t)
    row = pl.BlockSpec((tm, d), lambda i: (i, 0))
    vec = pl.BlockSpec((1, d), lambda i: (0, 0))
    return pl.pallas_call(
        _ln_body,
        out_shape=(jax.ShapeDtypeStruct((t, d), F32), jax.ShapeDtypeStruct((t, d), BF16)),
        grid=(t // tm,),
        in_specs=[row, row, vec, vec],
        out_specs=(row, row),
        compiler_params=_cparams(1),
        name="residual_ln",
    )(h, mix, g.reshape(1, d), b.reshape(1, d))


POOL_HALO = 16


def _pool_body(u_ref, halo_ref, w_ref, scale_ref, o_ref, buf, *, ts, tiles_per_seq):
    i = pl.program_id(0)
    first = (i % tiles_per_seq) == 0
    buf[0:POOL_HALO, :] = jnp.where(first, 0.0, halo_ref[...])
    buf[POOL_HALO:POOL_HALO + ts, :] = u_ref[...]
    pos = (i % tiles_per_seq) * ts + lax.broadcasted_iota(jnp.int32, (ts, 1), 0)
    for g, w in enumerate(POOL_WINDOWS):
        c0, c1 = g * POOL_GROUP, (g + 1) * POOL_GROUP
        cur = buf[POOL_HALO:POOL_HALO + ts, c0:c1]
        acc = cur
        for j in range(1, w):
            acc = acc + buf[POOL_HALO - j:POOL_HALO - j + ts, c0:c1]
        count = jnp.minimum(pos + 1, w).astype(F32)
        d = acc / count - cur
        y = jnp.dot(d.astype(BF16), w_ref[g], preferred_element_type=F32)
        o_ref[:, c0:c1] = (y * scale_ref[:, c0:c1]).astype(o_ref.dtype)


def _pool(u_main, pool_w_bf, pool_scale, seq):
    t = u_main.shape[0]
    ts = min(512, seq)
    tiles_per_seq = seq // ts
    hb = ts // POOL_HALO
    return pl.pallas_call(
        functools.partial(_pool_body, ts=ts, tiles_per_seq=tiles_per_seq),
        out_shape=jax.ShapeDtypeStruct((t, POOL_WIDTH), BF16),
        grid=(t // ts,),
        in_specs=[
            pl.BlockSpec((ts, POOL_WIDTH), lambda i: (i, 0)),
            pl.BlockSpec((POOL_HALO, POOL_WIDTH), lambda i: (jnp.maximum(i * hb - 1, 0), 0)),
            pl.BlockSpec((len(POOL_WINDOWS), POOL_GROUP, POOL_GROUP), lambda i: (0, 0, 0)),
            pl.BlockSpec((1, POOL_WIDTH), lambda i: (0, 0)),
        ],
        out_specs=pl.BlockSpec((ts, POOL_WIDTH), lambda i: (i, 0)),
        scratch_shapes=[pltpu.VMEM((ts + POOL_HALO, POOL_WIDTH), F32)],
        compiler_params=_cparams(1),
        name="pool_mixer",
    )(u_main, u_main, pool_w_bf, pool_scale.reshape(1, POOL_WIDTH))


CONV_HALO = 8
PAIR = 2 * SSM_HEAD_DIM
N_PAIRS = SSM_HEADS // 2
PAIRS_PER_GROUP = N_PAIRS // SSM_GROUPS


def _softplus(x):
    return jnp.maximum(x, 0.0) + jnp.log1p(jnp.exp(-jnp.abs(x)))


def _silu(x):
    return x * jax.nn.sigmoid(x)


def _ssd_body(z_ref, xs_ref, bc_ref, dt_ref, convw_ref, convb_ref, dtb_row_ref, dtb_col_ref,
              alog_row_ref, alog_col_ref, dskip_ref, normw_ref, o_ref,
              buf, xbc, carry, state, ybuf):
    L = SSM_CHUNK
    c = pl.program_id(1)

    @pl.when(c == 0)
    def _():
        carry[...] = jnp.zeros_like(carry)
        state[...] = jnp.zeros_like(state)

    buf[0:CONV_HALO, :] = carry[...]
    buf[CONV_HALO:CONV_HALO + L, 0:SSM_INNER] = xs_ref[...]
    buf[CONV_HALO:CONV_HALO + L, SSM_INNER:SSM_CONV_DIM] = bc_ref[...]
    carry[...] = buf[L:L + CONV_HALO, :]
    cw = 512
    for j in range(SSM_CONV_DIM // cw):
        cs = slice(j * cw, (j + 1) * cw)
        acc = convb_ref[:, cs] + buf[CONV_HALO - 3:CONV_HALO - 3 + L, cs] * convw_ref[0:1, cs]
        for i in range(1, SSM_CONV):
            off = CONV_HALO - 3 + i
            acc = acc + buf[off:off + L, cs] * convw_ref[i:i + 1, cs]
        xbc[:, cs] = _silu(acc)

    ri = lax.broadcasted_iota(jnp.int32, (L, L), 0)
    ci = lax.broadcasted_iota(jnp.int32, (L, L), 1)
    tril = ri >= ci
    dt_blk = dt_ref[...]
    dt = _softplus(dt_blk + dtb_row_ref[...])
    da = dt * (-jnp.exp(alog_row_ref[...]))
    acum = jnp.dot(tril.astype(F32), da, precision=lax.Precision.HIGHEST,
                   preferred_element_type=F32)
    dt_t = _softplus(dt_blk.T + dtb_col_ref[...])
    da_t = dt_t * (-jnp.exp(alog_col_ref[...]))
    acum_t = jnp.dot(da_t, (ri <= ci).astype(F32), precision=lax.Precision.HIGHEST,
                     preferred_element_type=F32)
    w_t = jnp.exp(acum_t[:, L - 1:L] - acum_t) * dt_t
    eac = jnp.exp(acum)
    cd_row = jnp.exp(acum[L - 1:L, :])

    lane = lax.broadcasted_iota(jnp.int32, (L, PAIR), 1)
    lane_row = lax.broadcasted_iota(jnp.int32, (1, PAIR), 1)
    nt = (((1,), (1,)), ((), ()))
    for g in range(SSM_GROUPS):
        b0 = SSM_INNER + g * SSM_STATE
        c0 = SSM_INNER + SSM_GROUPS * SSM_STATE + g * SSM_STATE
        bm = xbc[:, b0:b0 + SSM_STATE]
        cm = xbc[:, c0:c0 + SSM_STATE]
        cb = lax.dot_general(cm.astype(BF16), bm.astype(BF16), nt, preferred_element_type=F32)
        bm_t = bm.T
        for j in range(PAIRS_PER_GROUP):
            p = g * PAIRS_PER_GROUP + j
            xs_pair = xbc[:, p * PAIR:(p + 1) * PAIR].astype(BF16)
            st = state[p]
            lhs_y, lhs_s = [], []
            for h in (2 * p, 2 * p + 1):
                seg = acum[:, h:h + 1] - acum_t[h:h + 1, :]
                dec = jnp.exp(jnp.where(tril, seg, -jnp.inf))
                m_h = cb * dec * dt_t[h:h + 1, :]
                e_h = eac[:, h:h + 1] * cm
                lhs_y.append(jnp.concatenate([m_h, e_h], axis=1).astype(BF16))
                lhs_s.append((bm_t * w_t[h:h + 1, :]).astype(BF16))
            rhs = jnp.concatenate([xs_pair, st.astype(BF16)], axis=0)
            r = jnp.dot(jnp.concatenate(lhs_y, axis=0), rhs, preferred_element_type=F32)
            ybuf[:, p * PAIR:(p + 1) * PAIR] = jnp.where(lane < SSM_HEAD_DIM, r[0:L], r[L:2 * L])
            s_new = jnp.dot(jnp.concatenate(lhs_s, axis=0), xs_pair, preferred_element_type=F32)
            contrib = jnp.where(lane < SSM_HEAD_DIM, s_new[0:L], s_new[L:2 * L])
            cdp = jnp.where(lane_row < SSM_HEAD_DIM, cd_row[:, 2 * p:2 * p + 1], cd_row[:, 2 * p + 1:2 * p + 2])
            state[p] = st * cdp + contrib

    gw = SSM_INNER // SSM_GROUPS
    for g in range(SSM_GROUPS):
        cs = slice(g * gw, (g + 1) * gw)
        y = ybuf[:, cs] + xbc[:, cs] * dskip_ref[:, cs]
        y = y * _silu(z_ref[:, cs])
        ms = jnp.mean(y * y, axis=-1, keepdims=True)
        o_ref[:, cs] = (y * lax.rsqrt(ms + RMS_EPS) * normw_ref[:, cs]).astype(o_ref.dtype)


def _ssd(u_main, dt_raw, conv_w, conv_b, dt_bias, a_log, d_skip, norm_w, batch, seq):
    t = u_main.shape[0]
    L = SSM_CHUNK
    nc = seq // L
    pad = LANES - SSM_HEADS
    dtb = jnp.pad(dt_bias.astype(F32), (0, pad))
    alog = jnp.pad(a_log.astype(F32), (0, pad))
    dskip = jnp.repeat(d_skip.astype(F32), SSM_HEAD_DIM).reshape(1, SSM_INNER)
    zcol = POOL_WIDTH // SSM_INNER
    xcol = (POOL_WIDTH + SSM_INNER) // SSM_INNER
    bc_w = 2 * SSM_GROUPS * SSM_STATE
    bcol = (POOL_WIDTH + 2 * SSM_INNER) // bc_w
    rowmap = lambda col: (lambda b, c: (b * nc + c, col))
    full = lambda shape: pl.BlockSpec(shape, lambda b, c: (0,) * len(shape))
    return pl.pallas_call(
        _ssd_body,
        out_shape=jax.ShapeDtypeStruct((t, SSM_INNER), BF16),
        grid=(batch, nc),
        in_specs=[
            pl.BlockSpec((L, SSM_INNER), rowmap(zcol)),
            pl.BlockSpec((L, SSM_INNER), rowmap(xcol)),
            pl.BlockSpec((L, bc_w), rowmap(bcol)),
            pl.BlockSpec((L, LANES), rowmap(0)),
            full((SSM_CONV, SSM_CONV_DIM)),
            full((1, SSM_CONV_DIM)),
            full((1, LANES)),
            full((LANES, 1)),
            full((1, LANES)),
            full((LANES, 1)),
            full((1, SSM_INNER)),
            full((1, SSM_INNER)),
        ],
        out_specs=pl.BlockSpec((L, SSM_INNER), rowmap(0)),
        scratch_shapes=[
            pltpu.VMEM((L + CONV_HALO, SSM_CONV_DIM), F32),
            pltpu.VMEM((L, SSM_CONV_DIM), F32),
            pltpu.VMEM((CONV_HALO, SSM_CONV_DIM), F32),
            pltpu.VMEM((N_PAIRS, SSM_STATE, PAIR), F32),
            pltpu.VMEM((L, SSM_INNER), F32),
        ],
        compiler_params=_cparams(2),
        name="ssd_mixer",
    )(u_main, u_main, u_main, dt_raw, conv_w.astype(F32), conv_b.reshape(1, SSM_CONV_DIM).astype(F32),
      dtb.reshape(1, LANES), dtb.reshape(LANES, 1), alog.reshape(1, LANES), alog.reshape(LANES, 1),
      dskip, norm_w.reshape(1, SSM_INNER).astype(F32))


def _t5_bucket(dist):
    max_exact = REL_BUCKETS // 2
    d = np.maximum(dist, 1).astype(np.float32)
    large = max_exact + (np.log(d / max_exact) / np.log(REL_MAX_DIST / max_exact)
                         * (REL_BUCKETS - max_exact)).astype(np.int32)
    large = np.minimum(large, REL_BUCKETS - 1)
    return np.where(dist < max_exact, dist, large).astype(np.int32)


def _attn_bias_tables(rel_bias):
    blk = ATTN_BLOCK
    delta = blk + np.arange(blk)[:, None] - np.arange(2 * blk)[None, :]
    tabs = []
    for window, dil in DILATED:
        n_back = window // dil
        assert n_back <= blk
        band = (delta >= 0) & (delta <= n_back)
        rel = rel_bias[_t5_bucket(np.clip(delta, 0, n_back) * dil)]
        rel = jnp.transpose(rel, (2, 0, 1)).astype(F32)
        tabs.append(jnp.where(band[None], rel, NEG))
    return jnp.stack(tabs, 0)


ATTN_GROUP = 8


def _attn_body(q_ref, kp_ref, kc_ref, vp_ref, vc_ref, bias_ref, o_ref, m_ref, l_ref, acc_ref):
    blk = ATTN_BLOCK
    grp = ATTN_GROUP
    seq_first = pl.program_id(2) == 0
    nt = (((1,), (1,)), ((), ()))
    col = lax.broadcasted_iota(jnp.int32, (blk, 2 * blk), 1)

    def rows(start, n, d):
        if d > 1:
            return pl.ds(start, n, stride=d)
        return pl.ds(start if isinstance(start, int) else pl.multiple_of(start, blk), n)

    for bi, (_, d) in enumerate(DILATED):
        span = blk * d
        nblk = ATTN_SUPER // span
        bias = bias_ref[bi]
        bias_first = jnp.where(jnp.logical_and(seq_first, col < blk), NEG, bias)

        def block(qs, first, bi=bi, d=d, span=span, bias=bias, bias_first=bias_first):
            qr = rows(qs, blk, d)
            q = q_ref[qr, :].astype(BF16)
            if first:
                pr = rows(ATTN_SUPER - span + qs, blk, d)
                k = jnp.concatenate([kp_ref[pr, :], kc_ref[qr, :]], axis=0)
                v = jnp.concatenate([vp_ref[pr, :], vc_ref[qr, :]], axis=0)
                b = bias_first
            else:
                wr = rows(qs - span, 2 * blk, d)
                k = kc_ref[wr, :]
                v = vc_ref[wr, :]
                b = bias
            s = lax.dot_general(q, k.astype(BF16), nt, preferred_element_type=F32) + b
            rm = jnp.max(s, axis=1, keepdims=True)
            p = jnp.exp(s - rm)
            rs = jnp.sum(p, axis=1, keepdims=True)
            pv = jnp.dot(p.astype(BF16), v.astype(BF16), preferred_element_type=F32)
            m_ref[bi, qr, :] = jnp.broadcast_to(rm, (blk, LANES))
            l_ref[bi, qr, :] = jnp.broadcast_to(rs, (blk, LANES))
            acc_ref[bi, qr, :] = pv

        if nblk == 1:
            def body(i, c, block=block):
                for u in range(grp):
                    block(i * grp + u, True)
                return c
            lax.fori_loop(0, d // grp, body, 0)
        elif d == 1:
            block(0, True)
            for u in range(1, grp):
                block(u * span, False)

            def body(i, c, block=block, span=span):
                for u in range(grp):
                    block((i * grp + u) * span, False)
                return c
            lax.fori_loop(1, nblk // grp, body, 0)
        else:
            rpi = max(grp // nblk, 1)

            def body(i, c, block=block, span=span, nblk=nblk, rpi=rpi):
                for u in range(rpi):
                    r = i * rpi + u
                    block(r, True)
                    for n in range(1, nblk):
                        block(n * span + r, False)
                return c
            lax.fori_loop(0, d // rpi, body, 0)

    def merge(c, carry):
        rr = pl.ds(pl.multiple_of(c * blk, blk), blk)
        ms = [m_ref[i, rr, :] for i in range(len(DILATED))]
        mm = functools.reduce(jnp.maximum, ms)
        ws = [jnp.exp(m - mm) for m in ms]
        den = functools.reduce(lambda a, b: a + b, [w * l_ref[i, rr, :] for i, w in enumerate(ws)])
        num = functools.reduce(lambda a, b: a + b, [w * acc_ref[i, rr, :] for i, w in enumerate(ws)])
        o_ref[rr, :] = (num / den).astype(o_ref.dtype)
        return carry

    lax.fori_loop(0, ATTN_SUPER // blk, merge, 0)


def _attention(qkv_heads, bias_tabs, batch, seq):
    t = qkv_heads.shape[1]
    hn = ATTN_HEADS
    sup = ATTN_SUPER
    nst = seq // sup
    assert seq % sup == 0
    cur = lambda off: (lambda b, h, s: (off + h, b * nst + s, 0))
    prev = lambda off: (lambda b, h, s: (off + h, b * nst + jnp.maximum(s - 1, 0), 0))
    slab = lambda imap: pl.BlockSpec((None, sup, LANES), imap)
    return pl.pallas_call(
        _attn_body,
        out_shape=jax.ShapeDtypeStruct((t, D_MODEL), BF16),
        grid=(batch, hn, nst),
        in_specs=[
            slab(cur(0)),
            slab(prev(hn)), slab(cur(hn)),
            slab(prev(2 * hn)), slab(cur(2 * hn)),
            pl.BlockSpec((len(DILATED), None, ATTN_BLOCK, 2 * ATTN_BLOCK), lambda b, h, s: (0, h, 0, 0)),
        ],
        out_specs=pl.BlockSpec((sup, LANES), lambda b, h, s: (b * nst + s, h)),
        scratch_shapes=[pltpu.VMEM((len(DILATED), sup, LANES), F32)] * 3,
        compiler_params=_cparams(3),
        name="dilated_attention",
    )(qkv_heads, qkv_heads, qkv_heads, qkv_heads, qkv_heads, bias_tabs)


def _router_body(h_ref, w_ref, b_ref, idx_ref, gate_ref):
    logits = jnp.dot(h_ref[...], w_ref[...], preferred_element_type=F32) + b_ref[...]
    lane = lax.broadcasted_iota(jnp.int32, logits.shape, 1)
    vals, idxs = [], []
    for _ in range(TOP_K):
        m = jnp.max(logits, axis=1, keepdims=True)
        idx = jnp.min(jnp.where(logits == m, lane, LANES), axis=1, keepdims=True)
        vals.append(m)
        idxs.append(idx)
        logits = jnp.where(lane == idx, -jnp.inf, logits)
    es = [jnp.exp(v - vals[0]) for v in vals]
    den = es[0]
    for e in es[1:]:
        den = den + e
    idx_out = jnp.zeros(logits.shape, jnp.int32)
    gate_out = jnp.zeros(logits.shape, F32)
    for k in range(TOP_K):
        idx_out = jnp.where(lane == k, idxs[k], idx_out)
        gate_out = jnp.where(lane == k, es[k] / den, gate_out)
    idx_ref[...] = idx_out
    gate_ref[...] = gate_out


def _router(h_bf, router_w, router_b):
    t, d = h_bf.shape
    tm = min(1024, t)
    pad = LANES - N_EXPERTS
    w = jnp.pad(router_w, ((0, 0), (0, pad))).astype(BF16)
    b = jnp.pad(router_b.astype(F32), (0, pad), constant_values=NEG).reshape(1, LANES)
    idx, gates = pl.pallas_call(
        _router_body,
        out_shape=(jax.ShapeDtypeStruct((t, LANES), jnp.int32), jax.ShapeDtypeStruct((t, LANES), F32)),
        grid=(t // tm,),
        in_specs=[pl.BlockSpec((tm, d), lambda i: (i, 0)),
                  pl.BlockSpec((d, LANES), lambda i: (0, 0)),
                  pl.BlockSpec((1, LANES), lambda i: (0, 0))],
        out_specs=(pl.BlockSpec((tm, LANES), lambda i: (i, 0)), pl.BlockSpec((tm, LANES), lambda i: (i, 0))),
        compiler_params=_cparams(1),
        name="moe_router",
    )(h_bf, w, b)
    return idx, gates


def _route_plan(top_idx, tm):
    t = top_idx.shape[0]
    a = t * TOP_K
    e = top_idx[:, :TOP_K].reshape(a)
    onehot = (e[:, None] == jnp.arange(N_EXPERTS, dtype=jnp.int32)[None, :]).astype(jnp.int32)
    csum = jnp.cumsum(onehot, axis=0)
    rank = jnp.sum(csum * onehot, axis=1) - 1
    counts = csum[-1]
    padded = ((counts + tm - 1) // tm) * tm
    ends = jnp.cumsum(padded)
    starts = ends - padded
    slot = (starts[e] + rank).astype(jnp.int32)
    p = a + N_EXPERTS * tm
    token = (jnp.arange(a, dtype=jnp.int32) // TOP_K)
    row_src = jnp.zeros((p,), jnp.int32).at[slot].set(token, unique_indices=True)
    tile_start = jnp.arange(p // tm, dtype=jnp.int32) * tm
    tile_expert = jnp.minimum(jnp.searchsorted(ends, tile_start, side="right"), N_EXPERTS - 1).astype(jnp.int32)
    n_used = (ends[-1] // tm).astype(jnp.int32).reshape(1)
    return slot, row_src, tile_expert, n_used


def _deinterleave_body(w_ref, perm_ref, o_ref):
    o_ref[...] = jnp.dot(w_ref[...].astype(BF16), perm_ref[...], preferred_element_type=F32).astype(o_ref.dtype)


def _gate_up_weights(w_gate_up):
    ne, d, n = w_gate_up.shape
    tk = 1024
    src = np.concatenate([np.arange(0, n, 2), np.arange(1, n, 2)])
    perm = np.zeros((n, n), np.float32)
    perm[src, np.arange(n)] = 1.0
    return pl.pallas_call(
        _deinterleave_body,
        out_shape=jax.ShapeDtypeStruct((ne, d, n), BF16),
        grid=(ne, d // tk),
        in_specs=[pl.BlockSpec((None, tk, n), lambda e, k: (e, k, 0)),
                  pl.BlockSpec((n, n), lambda e, k: (0, 0))],
        out_specs=pl.BlockSpec((None, tk, n), lambda e, k: (e, k, 0)),
        compiler_params=_cparams(2),
        name="moe_gate_up_prep",
    )(w_gate_up, jnp.asarray(perm, BF16))


def _expert_body(te_ref, nused_ref, src_ref, h_ref, wgu_ref, bgu_ref, wd_ref, bd_ref, o_ref,
                 xbuf, sem, *, tm):
    i = pl.program_id(0)
    n_used = nused_ref[0]

    def gather(tile, buf):
        base = tile * tm

        def issue(r, c):
            tok = src_ref[base + r]
            pltpu.make_async_copy(h_ref.at[pl.ds(tok, 1)], xbuf.at[buf, pl.ds(r, 1)], sem.at[buf]).start()
            return c

        lax.fori_loop(0, tm, issue, 0, unroll=8)

    @pl.when(i == 0)
    def _():
        gather(0, 0)

    @pl.when(i + 1 < n_used)
    def _():
        gather(i + 1, (i + 1) % 2)

    @pl.when(i < n_used)
    def _():
        buf = i % 2
        pltpu.make_async_copy(h_ref.at[pl.ds(0, tm)], xbuf.at[buf], sem.at[buf]).wait()
        x = xbuf[buf].astype(BF16)
        hdn = jnp.dot(x, wgu_ref[...], preferred_element_type=F32) + bgu_ref[...]
        gate = jnp.minimum(hdn[:, :D_EXPERT], SWIGLU_LIMIT)
        up = jnp.clip(hdn[:, D_EXPERT:], -SWIGLU_LIMIT, SWIGLU_LIMIT)
        act = (up + 1.0) * gate * jax.nn.sigmoid(SWIGLU_ALPHA * gate)
        o_ref[...] = jnp.dot(act.astype(BF16), wd_ref[...], preferred_element_type=F32) + bd_ref[...]

    @pl.when(i >= n_used)
    def _():
        o_ref[...] = jnp.zeros_like(o_ref)


def _experts(h, row_src, tile_expert, n_used, wgu, bgu, wd, bd, tm):
    p = row_src.shape[0]
    d = h.shape[1]
    return pl.pallas_call(
        functools.partial(_expert_body, tm=tm),
        out_shape=jax.ShapeDtypeStruct((p, d), F32),
        grid_spec=pltpu.PrefetchScalarGridSpec(
            num_scalar_prefetch=3,
            grid=(p // tm,),
            in_specs=[
                pl.BlockSpec(memory_space=pl.ANY),
                pl.BlockSpec((None, d, 2 * D_EXPERT), lambda i, te, nu, src: (te[i], 0, 0)),
                pl.BlockSpec((None, 1, 2 * D_EXPERT), lambda i, te, nu, src: (te[i], 0, 0)),
                pl.BlockSpec((None, D_EXPERT, d), lambda i, te, nu, src: (te[i], 0, 0)),
                pl.BlockSpec((None, 1, d), lambda i, te, nu, src: (te[i], 0, 0)),
            ],
            out_specs=pl.BlockSpec((tm, d), lambda i, te, nu, src: (i, 0)),
            scratch_shapes=[pltpu.VMEM((2, tm, d), F32), pltpu.SemaphoreType.DMA((2,))],
        ),
        compiler_params=_cparams(1),
        name="moe_experts",
    )(tile_expert, n_used, row_src, h, wgu, bgu, wd, bd)


def _combine_body(slot_ref, y_ref, gate_ref, h_ref, g_ref, b_ref, o_ref, obf_ref, ybuf, sem, *, tc):
    i = pl.program_id(0)
    n = pl.num_programs(0)

    def gather(tile, buf):
        base = tile * tc

        def issue(r, c):
            for k in range(TOP_K):
                s = slot_ref[(base + r) * TOP_K + k]
                pltpu.make_async_copy(y_ref.at[pl.ds(s, 1)], ybuf.at[buf, k, pl.ds(r, 1)], sem.at[buf]).start()
            return c

        lax.fori_loop(0, tc, issue, 0, unroll=4)

    @pl.when(i == 0)
    def _():
        gather(0, 0)

    @pl.when(i + 1 < n)
    def _():
        gather(i + 1, (i + 1) % 2)

    buf = i % 2
    for k in range(TOP_K):
        pltpu.make_async_copy(y_ref.at[pl.ds(0, tc)], ybuf.at[buf, k], sem.at[buf]).wait()
    gates = gate_ref[...]
    ffn = gates[:, 0:1] * ybuf[buf, 0]
    for k in range(1, TOP_K):
        ffn = ffn + gates[:, k:k + 1] * ybuf[buf, k]
    y = _layer_norm_rows(DEEPNORM_ALPHA * h_ref[...] + ffn, g_ref[...], b_ref[...])
    o_ref[...] = y
    obf_ref[...] = y.astype(BF16)


def _combine_ln(slot, y_sorted, gates, h, g, b):
    t, d = h.shape
    tc = min(128, t)
    row = lambda i, s: (i, 0)
    vec = lambda i, s: (0, 0)
    return pl.pallas_call(
        functools.partial(_combine_body, tc=tc),
        out_shape=(jax.ShapeDtypeStruct((t, d), F32), jax.ShapeDtypeStruct((t, d), BF16)),
        grid_spec=pltpu.PrefetchScalarGridSpec(
            num_scalar_prefetch=1,
            grid=(t // tc,),
            in_specs=[
                pl.BlockSpec(memory_space=pl.ANY),
                pl.BlockSpec((tc, LANES), row),
                pl.BlockSpec((tc, d), row),
                pl.BlockSpec((1, d), vec),
                pl.BlockSpec((1, d), vec),
            ],
            out_specs=(pl.BlockSpec((tc, d), row), pl.BlockSpec((tc, d), row)),
            scratch_shapes=[pltpu.VMEM((2, TOP_K, tc, d), F32), pltpu.SemaphoreType.DMA((2,))],
        ),
        compiler_params=_cparams(1),
        name="moe_combine_ln",
    )(slot, y_sorted, gates, h, g.reshape(1, d), b.reshape(1, d))


def _moe_ln(h, h_bf, router_w, router_b, w_gate_up, b_gate_up, w_down, b_down, g, b):
    tm = MOE_TILE
    top_idx, gates = _router(h_bf, router_w, router_b)
    slot, row_src, tile_expert, n_used = _route_plan(top_idx, tm)
    wgu = _gate_up_weights(w_gate_up)
    bgu = jnp.concatenate([b_gate_up[..., 0::2], b_gate_up[..., 1::2]], axis=-1).astype(F32)
    bgu = bgu.reshape(N_EXPERTS, 1, 2 * D_EXPERT)
    wd = w_down.astype(BF16)
    bd = b_down.astype(F32).reshape(N_EXPERTS, 1, D_MODEL)
    y_sorted = _experts(h, row_src, tile_expert, n_used, wgu, bgu, wd, bd, tm)
    return _combine_ln(slot, y_sorted, gates, h, g, b)


def _even_mixer(h_bf, w_in, pool_w, pool_scale, conv_w, conv_b, dt_bias, a_log, d_skip, norm_w, w_out,
                batch, seq):
    w_in_bf = w_in.astype(BF16)
    w_dt = jnp.pad(w_in[:, EVEN_MAIN:], ((0, 0), (0, LANES - SSM_HEADS))).astype(BF16)
    u_main = _matmul([h_bf], w_in_bf, EVEN_MAIN, 512, F32, "even_in_proj")
    dt_raw = _matmul([h_bf], w_dt, LANES, LANES, F32, "even_dt_proj")
    y_pool = _pool(u_main, pool_w.astype(BF16), pool_scale.astype(F32), seq)
    y_ssd = _ssd(u_main, dt_raw, conv_w, conv_b, dt_bias, a_log, d_skip, norm_w, batch, seq)
    return _matmul([y_pool, y_ssd], w_out.astype(BF16), D_MODEL, 512, F32, "even_out_proj")


def _attn_mixer(h_bf, w_qkv, w_out, bias_tabs, batch, seq):
    qkv_heads = _matmul([h_bf], w_qkv.astype(BF16), 3 * D_MODEL, 512, F32, "attn_qkv_proj", head_out=True,
                        scaled_cols=D_MODEL, scale=ATTN_HEAD_DIM ** -0.5)
    o = _attention(qkv_heads, bias_tabs, batch, seq)
    return _matmul([o], w_out.astype(BF16), D_MODEL, 512, F32, "attn_out_proj")


def kernel(x, rel_bias, even_w_in, pool_w, pool_scale, conv_w, conv_b, dt_bias, a_log, d_skip, ssm_norm_w, even_w_out, attn_w_qkv, attn_w_out, ln1_g, ln1_b, ln2_g, ln2_b, router_w, router_b, w_gate_up, b_gate_up, w_down, b_down):
    batch, seq, d = x.shape
    h = x.reshape(batch * seq, d).astype(F32)
    h_bf = h.astype(BF16)
    bias_tabs = _attn_bias_tables(rel_bias)
    for layer in range(DEPTH):
        i = layer // 2
        if layer % 2 == 0:
            mix = _even_mixer(h_bf, even_w_in[i], pool_w[i], pool_scale[i], conv_w[i], conv_b[i],
                              dt_bias[i], a_log[i], d_skip[i], ssm_norm_w[i], even_w_out[i], batch, seq)
        else:
            mix = _attn_mixer(h_bf, attn_w_qkv[i], attn_w_out[i], bias_tabs, batch, seq)
        h, h_bf = _residual_ln(h, mix, ln1_g[layer], ln1_b[layer])
        h, h_bf = _moe_ln(h, h_bf, router_w[layer], router_b[layer], w_gate_up[layer], b_gate_up[layer],
                          w_down[layer], b_down[layer], ln2_g[layer], ln2_b[layer])
    return h.reshape(batch, seq, d).astype(x.dtype)
```

```python
import functools
import math

import numpy as np
import jax
import jax.numpy as jnp
from jax import lax
from jax.experimental import pallas as pl
from jax.experimental.pallas import tpu as pltpu

F32 = jnp.float32
BF16 = jnp.bfloat16

D_MODEL = 4096
DEPTH = 4
POOL_WINDOWS = (2, 4, 8, 16)
POOL_GROUP = D_MODEL // 8
POOL_WIDTH = POOL_GROUP * len(POOL_WINDOWS)
SSM_HEAD_DIM = 64
SSM_INNER = D_MODEL // 2
SSM_HEADS = SSM_INNER // SSM_HEAD_DIM
SSM_GROUPS = 4
SSM_STATE = 128
SSM_CONV = 4
SSM_CHUNK = 128
SSM_CONV_DIM = SSM_INNER + 2 * SSM_GROUPS * SSM_STATE
EVEN_MAIN = POOL_WIDTH + SSM_INNER + SSM_CONV_DIM
ATTN_HEAD_DIM = 128
ATTN_HEADS = D_MODEL // ATTN_HEAD_DIM
DILATED = ((128, 1), (512, 4), (2048, 16))
ATTN_BLOCK = 128
ATTN_SUPER = ATTN_BLOCK * 16
REL_BUCKETS = 32
REL_MAX_DIST = 2048
N_EXPERTS = 32
TOP_K = 4
D_EXPERT = 256
SWIGLU_LIMIT = 7.0
SWIGLU_ALPHA = 1.702
DEEPNORM_ALPHA = (2 * DEPTH) ** 0.25
LN_EPS = 1e-5
RMS_EPS = 1e-5

LANES = 128
NEG = -1e30
VMEM_LIMIT = 56 * 1024 * 1024
MOE_TILE = 256


def _cparams(n_axes):
    return pltpu.CompilerParams(dimension_semantics=("arbitrary",) * n_axes,
                                vmem_limit_bytes=VMEM_LIMIT)


def _mm_body(*refs, n_lhs, k_bounds, head_out, scaled_blocks, scale):
    lhs = refs[:n_lhs]
    w_ref = refs[n_lhs]
    o_ref = refs[n_lhs + 1]
    acc = None
    for l_ref, (k0, k1) in zip(lhs, k_bounds):
        part = jnp.dot(l_ref[...], w_ref[k0:k1, :], preferred_element_type=F32)
        acc = part if acc is None else acc + part
    if scaled_blocks:
        acc = acc * jnp.where(pl.program_id(1) < scaled_blocks, scale, 1.0)
    if head_out:
        for j in range(o_ref.shape[0]):
            o_ref[j] = acc[:, j * LANES:(j + 1) * LANES].astype(o_ref.dtype)
    else:
        o_ref[...] = acc.astype(o_ref.dtype)


def _matmul(lhs_list, w, n_out, tn, out_dtype, name, head_out=False, scaled_cols=0, scale=1.0):
    assert scaled_cols % tn == 0
    m = lhs_list[0].shape[0]
    k = w.shape[0]
    tm = min(1024, m)
    k_bounds = []
    k0 = 0
    for l in lhs_list:
        k_bounds.append((k0, k0 + l.shape[1]))
        k0 += l.shape[1]
    assert k0 == k and m % tm == 0 and n_out % tn == 0
    in_specs = [pl.BlockSpec((tm, l.shape[1]), lambda i, j: (i, 0)) for l in lhs_list]
    in_specs.append(pl.BlockSpec((k, tn), lambda i, j: (0, j)))
    if head_out:
        out_shape = jax.ShapeDtypeStruct((n_out // LANES, m, LANES), out_dtype)
        out_spec = pl.BlockSpec((tn // LANES, tm, LANES), lambda i, j: (j, i, 0))
    else:
        out_shape = jax.ShapeDtypeStruct((m, n_out), out_dtype)
        out_spec = pl.BlockSpec((tm, tn), lambda i, j: (i, j))
    return pl.pallas_call(
        functools.partial(_mm_body, n_lhs=len(lhs_list), k_bounds=tuple(k_bounds), head_out=head_out,
                          scaled_blocks=scaled_cols // tn, scale=scale),
        out_shape=out_shape,
        grid=(m // tm, n_out // tn),
        in_specs=in_specs,
        out_specs=out_spec,
        compiler_params=_cparams(2),
        name=name,
    )(*lhs_list, w)


def _cast_body(w_ref, o_ref):
    o_ref[...] = w_ref[...].astype(o_ref.dtype)


def _cast_rows_bf16(w2d, row0, nrows, name):
    n = w2d.shape[1]
    tk = 256
    assert row0 % tk == 0 and nrows % tk == 0
    return pl.pallas_call(
        _cast_body,
        out_shape=jax.ShapeDtypeStruct((nrows, n), BF16),
        grid=(nrows // tk,),
        in_specs=[pl.BlockSpec((tk, n), lambda i: (row0 // tk + i, 0))],
        out_specs=pl.BlockSpec((tk, n), lambda i: (i, 0)),
        compiler_params=_cparams(1),
        name=name,
    )(w2d)


def _layer_bf16(w_stacked, layer, name):
    n = w_stacked.shape[-1]
    rows = math.prod(w_stacked.shape[1:-1])
    return _cast_rows_bf16(w_stacked.reshape(-1, n), layer * rows, rows, name)


HALF = D_MODEL // 2


def _pack_pairs(y):
    lo = lax.bitcast_convert_type(y[:, :HALF].astype(BF16).astype(F32), jnp.uint32)
    hi = lax.bitcast_convert_type(y[:, HALF:].astype(BF16).astype(F32), jnp.uint32)
    return hi | (lo >> 16)


def _unpack_pairs(w):
    lo = lax.bitcast_convert_type(w << 16, F32)
    hi = lax.bitcast_convert_type(w & jnp.uint32(0xFFFF0000), F32)
    return lo, hi


def _layer_norm_rows(x, g, b):
    mu = jnp.mean(x, axis=-1, keepdims=True)
    xc = x - mu
    var = jnp.mean(xc * xc, axis=-1, keepdims=True)
    return xc * lax.rsqrt(var + LN_EPS) * g + b


def _ln_body(h_ref, mix_ref, g_ref, b_ref, o_ref, opk_ref):
    y = _layer_norm_rows(DEEPNORM_ALPHA * h_ref[...] + mix_ref[...], g_ref[...], b_ref[...])
    o_ref[...] = y
    opk_ref[...] = _pack_pairs(y)


def _residual_ln(h, mix, g, b):
    t, d = h.shape
    tm = min(256, t)
    row = pl.BlockSpec((tm, d), lambda i: (i, 0))
    vec = pl.BlockSpec((1, d), lambda i: (0, 0))
    return pl.pallas_call(
        _ln_body,
        out_shape=(jax.ShapeDtypeStruct((t, d), F32), jax.ShapeDtypeStruct((t, HALF), jnp.uint32)),
        grid=(t // tm,),
        in_specs=[row, row, vec, vec],
        out_specs=(row, pl.BlockSpec((tm, HALF), lambda i: (i, 0))),
        compiler_params=_cparams(1),
        name="residual_ln",
    )(h, mix, g.reshape(1, d), b.reshape(1, d))


POOL_HALO = 16


def _pool_body(u_ref, halo_ref, w_ref, scale_ref, o_ref, buf, *, ts, tiles_per_seq):
    i = pl.program_id(0)
    first = (i % tiles_per_seq) == 0
    buf[0:POOL_HALO, :] = jnp.where(first, 0.0, halo_ref[...])
    buf[POOL_HALO:POOL_HALO + ts, :] = u_ref[...]
    pos = (i % tiles_per_seq) * ts + lax.broadcasted_iota(jnp.int32, (ts, 1), 0)
    for g, w in enumerate(POOL_WINDOWS):
        c0, c1 = g * POOL_GROUP, (g + 1) * POOL_GROUP
        cur = buf[POOL_HALO:POOL_HALO + ts, c0:c1]
        acc = cur
        for j in range(1, w):
            acc = acc + buf[POOL_HALO - j:POOL_HALO - j + ts, c0:c1]
        count = jnp.minimum(pos + 1, w).astype(F32)
        d = acc / count - cur
        y = jnp.dot(d.astype(BF16), w_ref[g], preferred_element_type=F32)
        o_ref[:, c0:c1] = (y * scale_ref[:, c0:c1]).astype(o_ref.dtype)


def _pool(u_main, pool_w_bf, pool_scale, seq):
    t = u_main.shape[0]
    ts = min(512, seq)
    tiles_per_seq = seq // ts
    hb = ts // POOL_HALO
    return pl.pallas_call(
        functools.partial(_pool_body, ts=ts, tiles_per_seq=tiles_per_seq),
        out_shape=jax.ShapeDtypeStruct((t, POOL_WIDTH), BF16),
        grid=(t // ts,),
        in_specs=[
            pl.BlockSpec((ts, POOL_WIDTH), lambda i: (i, 0)),
            pl.BlockSpec((POOL_HALO, POOL_WIDTH), lambda i: (jnp.maximum(i * hb - 1, 0), 0)),
            pl.BlockSpec((len(POOL_WINDOWS), POOL_GROUP, POOL_GROUP), lambda i: (0, 0, 0)),
            pl.BlockSpec((1, POOL_WIDTH), lambda i: (0, 0)),
        ],
        out_specs=pl.BlockSpec((ts, POOL_WIDTH), lambda i: (i, 0)),
        scratch_shapes=[pltpu.VMEM((ts + POOL_HALO, POOL_WIDTH), F32)],
        compiler_params=_cparams(1),
        name="pool_mixer",
    )(u_main, u_main, pool_w_bf, pool_scale.reshape(1, POOL_WIDTH))


CONV_HALO = 8
PAIR = 2 * SSM_HEAD_DIM
N_PAIRS = SSM_HEADS // 2
PAIRS_PER_GROUP = N_PAIRS // SSM_GROUPS


def _softplus(x):
    return jnp.maximum(x, 0.0) + jnp.log1p(jnp.exp(-jnp.abs(x)))


def _silu(x):
    return x * jax.nn.sigmoid(x)


def _ssd_body(z_ref, xs_ref, bc_ref, dt_ref, convw_ref, convb_ref, dtb_row_ref, dtb_col_ref,
              alog_row_ref, alog_col_ref, dskip_ref, normw_ref, o_ref,
              buf, xbc, carry, state, ybuf):
    L = SSM_CHUNK
    c = pl.program_id(1)

    @pl.when(c == 0)
    def _():
        carry[...] = jnp.zeros_like(carry)
        state[...] = jnp.zeros_like(state)

    buf[0:CONV_HALO, :] = carry[...]
    buf[CONV_HALO:CONV_HALO + L, 0:SSM_INNER] = xs_ref[...]
    buf[CONV_HALO:CONV_HALO + L, SSM_INNER:SSM_CONV_DIM] = bc_ref[...]
    carry[...] = buf[L:L + CONV_HALO, :]
    cw = 512
    for j in range(SSM_CONV_DIM // cw):
        cs = slice(j * cw, (j + 1) * cw)
        acc = convb_ref[:, cs] + buf[CONV_HALO - 3:CONV_HALO - 3 + L, cs] * convw_ref[0:1, cs]
        for i in range(1, SSM_CONV):
            off = CONV_HALO - 3 + i
            acc = acc + buf[off:off + L, cs] * convw_ref[i:i + 1, cs]
        xbc[:, cs] = _silu(acc)

    ri = lax.broadcasted_iota(jnp.int32, (L, L), 0)
    ci = lax.broadcasted_iota(jnp.int32, (L, L), 1)
    tril = ri >= ci
    dt_blk = dt_ref[...]
    dt = _softplus(dt_blk + dtb_row_ref[...])
    da = dt * (-jnp.exp(alog_row_ref[...]))
    acum = jnp.dot(tril.astype(F32), da, precision=lax.Precision.HIGHEST,
                   preferred_element_type=F32)
    dt_t = _softplus(dt_blk.T + dtb_col_ref[...])
    da_t = dt_t * (-jnp.exp(alog_col_ref[...]))
    acum_t = jnp.dot(da_t, (ri <= ci).astype(F32), precision=lax.Precision.HIGHEST,
                     preferred_element_type=F32)
    w_t = jnp.exp(acum_t[:, L - 1:L] - acum_t) * dt_t
    eac = jnp.exp(acum)
    cd_row = jnp.exp(acum[L - 1:L, :])

    lane = lax.broadcasted_iota(jnp.int32, (L, PAIR), 1)
    lane_row = lax.broadcasted_iota(jnp.int32, (1, PAIR), 1)
    nt = (((1,), (1,)), ((), ()))
    for g in range(SSM_GROUPS):
        b0 = SSM_INNER + g * SSM_STATE
        c0 = SSM_INNER + SSM_GROUPS * SSM_STATE + g * SSM_STATE
        bm = xbc[:, b0:b0 + SSM_STATE]
        cm = xbc[:, c0:c0 + SSM_STATE]
        cb = lax.dot_general(cm.astype(BF16), bm.astype(BF16), nt, preferred_element_type=F32)
        bm_t = bm.T
        for j in range(PAIRS_PER_GROUP):
            p = g * PAIRS_PER_GROUP + j
            xs_pair = xbc[:, p * PAIR:(p + 1) * PAIR].astype(BF16)
            st = state[p]
            lhs_y, lhs_s = [], []
            for h in (2 * p, 2 * p + 1):
                seg = acum[:, h:h + 1] - acum_t[h:h + 1, :]
                dec = jnp.exp(jnp.where(tril, seg, -jnp.inf))
                m_h = cb * dec * dt_t[h:h + 1, :]
                e_h = eac[:, h:h + 1] * cm
                lhs_y.append(jnp.concatenate([m_h, e_h], axis=1).astype(BF16))
                lhs_s.append((bm_t * w_t[h:h + 1, :]).astype(BF16))
            rhs = jnp.concatenate([xs_pair, st.astype(BF16)], axis=0)
            r = jnp.dot(jnp.concatenate(lhs_y, axis=0), rhs, preferred_element_type=F32)
            ybuf[:, p * PAIR:(p + 1) * PAIR] = jnp.where(lane < SSM_HEAD_DIM, r[0:L], r[L:2 * L])
            s_new = jnp.dot(jnp.concatenate(lhs_s, axis=0), xs_pair, preferred_element_type=F32)
            contrib = jnp.where(lane < SSM_HEAD_DIM, s_new[0:L], s_new[L:2 * L])
            cdp = jnp.where(lane_row < SSM_HEAD_DIM, cd_row[:, 2 * p:2 * p + 1], cd_row[:, 2 * p + 1:2 * p + 2])
            state[p] = st * cdp + contrib

    gw = SSM_INNER // SSM_GROUPS
    for g in range(SSM_GROUPS):
        cs = slice(g * gw, (g + 1) * gw)
        y = ybuf[:, cs] + xbc[:, cs] * dskip_ref[:, cs]
        y = y * _silu(z_ref[:, cs])
        ms = jnp.mean(y * y, axis=-1, keepdims=True)
        o_ref[:, cs] = (y * lax.rsqrt(ms + RMS_EPS) * normw_ref[:, cs]).astype(o_ref.dtype)


def _ssd(u_main, dt_raw, conv_w, conv_b, dt_bias, a_log, d_skip, norm_w, batch, seq):
    t = u_main.shape[0]
    L = SSM_CHUNK
    nc = seq // L
    pad = LANES - SSM_HEADS
    dtb = jnp.pad(dt_bias.astype(F32), (0, pad))
    alog = jnp.pad(a_log.astype(F32), (0, pad))
    dskip = jnp.repeat(d_skip.astype(F32), SSM_HEAD_DIM).reshape(1, SSM_INNER)
    zcol = POOL_WIDTH // SSM_INNER
    xcol = (POOL_WIDTH + SSM_INNER) // SSM_INNER
    bc_w = 2 * SSM_GROUPS * SSM_STATE
    bcol = (POOL_WIDTH + 2 * SSM_INNER) // bc_w
    rowmap = lambda col: (lambda b, c: (b * nc + c, col))
    full = lambda shape: pl.BlockSpec(shape, lambda b, c: (0,) * len(shape))
    return pl.pallas_call(
        _ssd_body,
        out_shape=jax.ShapeDtypeStruct((t, SSM_INNER), BF16),
        grid=(batch, nc),
        in_specs=[
            pl.BlockSpec((L, SSM_INNER), rowmap(zcol)),
            pl.BlockSpec((L, SSM_INNER), rowmap(xcol)),
            pl.BlockSpec((L, bc_w), rowmap(bcol)),
            pl.BlockSpec((L, LANES), rowmap(0)),
            full((SSM_CONV, SSM_CONV_DIM)),
            full((1, SSM_CONV_DIM)),
            full((1, LANES)),
            full((LANES, 1)),
            full((1, LANES)),
            full((LANES, 1)),
            full((1, SSM_INNER)),
            full((1, SSM_INNER)),
        ],
        out_specs=pl.BlockSpec((L, SSM_INNER), rowmap(0)),
        scratch_shapes=[
            pltpu.VMEM((L + CONV_HALO, SSM_CONV_DIM), F32),
            pltpu.VMEM((L, SSM_CONV_DIM), F32),
            pltpu.VMEM((CONV_HALO, SSM_CONV_DIM), F32),
            pltpu.VMEM((N_PAIRS, SSM_STATE, PAIR), F32),
            pltpu.VMEM((L, SSM_INNER), F32),
        ],
        compiler_params=_cparams(2),
        name="ssd_mixer",
    )(u_main, u_main, u_main, dt_raw, conv_w.astype(F32), conv_b.reshape(1, SSM_CONV_DIM).astype(F32),
      dtb.reshape(1, LANES), dtb.reshape(LANES, 1), alog.reshape(1, LANES), alog.reshape(LANES, 1),
      dskip, norm_w.reshape(1, SSM_INNER).astype(F32))


def _t5_bucket(dist):
    max_exact = REL_BUCKETS // 2
    d = np.maximum(dist, 1).astype(np.float32)
    large = max_exact + (np.log(d / max_exact) / np.log(REL_MAX_DIST / max_exact)
                         * (REL_BUCKETS - max_exact)).astype(np.int32)
    large = np.minimum(large, REL_BUCKETS - 1)
    return np.where(dist < max_exact, dist, large).astype(np.int32)


def _attn_bias_tables(rel_bias):
    blk = ATTN_BLOCK
    delta = blk + np.arange(blk)[:, None] - np.arange(2 * blk)[None, :]
    tabs = []
    for window, dil in DILATED:
        n_back = window // dil
        assert n_back <= blk
        band = (delta >= 0) & (delta <= n_back)
        rel = rel_bias[_t5_bucket(np.clip(delta, 0, n_back) * dil)]
        rel = jnp.transpose(rel, (2, 0, 1)).astype(F32)
        tabs.append(jnp.where(band[None], rel, NEG))
    return jnp.stack(tabs, 0)


ATTN_GROUP = 8


def _attn_body(q_ref, kp_ref, kc_ref, vp_ref, vc_ref, bias_ref, o_ref, m_ref, l_ref, acc_ref):
    blk = ATTN_BLOCK
    grp = ATTN_GROUP
    seq_first = pl.program_id(2) == 0
    nt = (((1,), (1,)), ((), ()))
    col = lax.broadcasted_iota(jnp.int32, (blk, 2 * blk), 1)

    def rows(start, n, d):
        if d > 1:
            return pl.ds(start, n, stride=d)
        return pl.ds(start if isinstance(start, int) else pl.multiple_of(start, blk), n)

    for bi, (_, d) in enumerate(DILATED):
        span = blk * d
        nblk = ATTN_SUPER // span
        bias = bias_ref[bi]
        bias_first = jnp.where(jnp.logical_and(seq_first, col < blk), NEG, bias)

        def block(qs, first, bi=bi, d=d, span=span, bias=bias, bias_first=bias_first):
            qr = rows(qs, blk, d)
            q = q_ref[qr, :].astype(BF16)
            if first:
                pr = rows(ATTN_SUPER - span + qs, blk, d)
                k = jnp.concatenate([kp_ref[pr, :], kc_ref[qr, :]], axis=0)
                v = jnp.concatenate([vp_ref[pr, :], vc_ref[qr, :]], axis=0)
                b = bias_first
            else:
                wr = rows(qs - span, 2 * blk, d)
                k = kc_ref[wr, :]
                v = vc_ref[wr, :]
                b = bias
            s = lax.dot_general(q, k.astype(BF16), nt, preferred_element_type=F32) + b
            rm = jnp.max(s, axis=1, keepdims=True)
            p = jnp.exp(s - rm)
            rs = jnp.sum(p, axis=1, keepdims=True)
            pv = jnp.dot(p.astype(BF16), v.astype(BF16), preferred_element_type=F32)
            m_ref[bi, qr, :] = jnp.broadcast_to(rm, (blk, LANES))
            l_ref[bi, qr, :] = jnp.broadcast_to(rs, (blk, LANES))
            acc_ref[bi, qr, :] = pv

        if nblk == 1:
            def body(i, c, block=block):
                for u in range(grp):
                    block(i * grp + u, True)
                return c
            lax.fori_loop(0, d // grp, body, 0)
        elif d == 1:
            block(0, True)
            for u in range(1, grp):
                block(u * span, False)

            def body(i, c, block=block, span=span):
                for u in range(grp):
                    block((i * grp + u) * span, False)
                return c
            lax.fori_loop(1, nblk // grp, body, 0)
        else:
            rpi = max(grp // nblk, 1)

            def body(i, c, block=block, span=span, nblk=nblk, rpi=rpi):
                for u in range(rpi):
                    r = i * rpi + u
                    block(r, True)
                    for n in range(1, nblk):
                        block(n * span + r, False)
                return c
            lax.fori_loop(0, d // rpi, body, 0)

    def merge(c, carry):
        rr = pl.ds(pl.multiple_of(c * blk, blk), blk)
        ms = [m_ref[i, rr, :] for i in range(len(DILATED))]
        mm = functools.reduce(jnp.maximum, ms)
        ws = [jnp.exp(m - mm) for m in ms]
        den = functools.reduce(lambda a, b: a + b, [w * l_ref[i, rr, :] for i, w in enumerate(ws)])
        num = functools.reduce(lambda a, b: a + b, [w * acc_ref[i, rr, :] for i, w in enumerate(ws)])
        o_ref[rr, :] = (num / den).astype(o_ref.dtype)
        return carry

    lax.fori_loop(0, ATTN_SUPER // blk, merge, 0)


def _attention(qkv_heads, bias_tabs, batch, seq):
    t = qkv_heads.shape[1]
    hn = ATTN_HEADS
    sup = ATTN_SUPER
    nst = seq // sup
    assert seq % sup == 0
    cur = lambda off: (lambda b, h, s: (off + h, b * nst + s, 0))
    prev = lambda off: (lambda b, h, s: (off + h, b * nst + jnp.maximum(s - 1, 0), 0))
    slab = lambda imap: pl.BlockSpec((None, sup, LANES), imap)
    return pl.pallas_call(
        _attn_body,
        out_shape=jax.ShapeDtypeStruct((t, D_MODEL), BF16),
        grid=(batch, hn, nst),
        in_specs=[
            slab(cur(0)),
            slab(prev(hn)), slab(cur(hn)),
            slab(prev(2 * hn)), slab(cur(2 * hn)),
            pl.BlockSpec((len(DILATED), None, ATTN_BLOCK, 2 * ATTN_BLOCK), lambda b, h, s: (0, h, 0, 0)),
        ],
        out_specs=pl.BlockSpec((sup, LANES), lambda b, h, s: (b * nst + s, h)),
        scratch_shapes=[pltpu.VMEM((len(DILATED), sup, LANES), F32)] * 3,
        compiler_params=_cparams(3),
        name="dilated_attention",
    )(qkv_heads, qkv_heads, qkv_heads, qkv_heads, qkv_heads, bias_tabs)


def _router_body(h_ref, w_ref, b_ref, idx_ref, gate_ref):
    lo, hi = _unpack_pairs(h_ref[...])
    logits = (jnp.dot(lo.astype(BF16), w_ref[:HALF, :], preferred_element_type=F32)
              + jnp.dot(hi.astype(BF16), w_ref[HALF:, :], preferred_element_type=F32) + b_ref[...])
    lane = lax.broadcasted_iota(jnp.int32, logits.shape, 1)
    vals, idxs = [], []
    for _ in range(TOP_K):
        m = jnp.max(logits, axis=1, keepdims=True)
        idx = jnp.min(jnp.where(logits == m, lane, LANES), axis=1, keepdims=True)
        vals.append(m)
        idxs.append(idx)
        logits = jnp.where(lane == idx, -jnp.inf, logits)
    es = [jnp.exp(v - vals[0]) for v in vals]
    den = es[0]
    for e in es[1:]:
        den = den + e
    idx_out = jnp.zeros(logits.shape, jnp.int32)
    gate_out = jnp.zeros(logits.shape, F32)
    for k in range(TOP_K):
        idx_out = jnp.where(lane == k, idxs[k], idx_out)
        gate_out = jnp.where(lane == k, es[k] / den, gate_out)
    idx_ref[...] = idx_out
    gate_ref[...] = gate_out


def _router(h_pk, router_w, router_b):
    t, hd = h_pk.shape
    d = 2 * hd
    tm = min(1024, t)
    pad = LANES - N_EXPERTS
    w = jnp.pad(router_w, ((0, 0), (0, pad))).astype(BF16)
    b = jnp.pad(router_b.astype(F32), (0, pad), constant_values=NEG).reshape(1, LANES)
    idx, gates = pl.pallas_call(
        _router_body,
        out_shape=(jax.ShapeDtypeStruct((t, LANES), jnp.int32), jax.ShapeDtypeStruct((t, LANES), F32)),
        grid=(t // tm,),
        in_specs=[pl.BlockSpec((tm, hd), lambda i: (i, 0)),
                  pl.BlockSpec((d, LANES), lambda i: (0, 0)),
                  pl.BlockSpec((1, LANES), lambda i: (0, 0))],
        out_specs=(pl.BlockSpec((tm, LANES), lambda i: (i, 0)), pl.BlockSpec((tm, LANES), lambda i: (i, 0))),
        compiler_params=_cparams(1),
        name="moe_router",
    )(h_pk, w, b)
    return idx, gates


def _route_plan(top_idx, tm):
    t = top_idx.shape[0]
    a = t * TOP_K
    e = top_idx[:, :TOP_K].reshape(a)
    onehot = (e[:, None] == jnp.arange(N_EXPERTS, dtype=jnp.int32)[None, :]).astype(jnp.int32)
    csum = jnp.cumsum(onehot, axis=0)
    rank = jnp.sum(csum * onehot, axis=1) - 1
    counts = csum[-1]
    padded = ((counts + tm - 1) // tm) * tm
    ends = jnp.cumsum(padded)
    starts = ends - padded
    slot = (starts[e] + rank).astype(jnp.int32)
    p = a + N_EXPERTS * tm
    token = (jnp.arange(a, dtype=jnp.int32) // TOP_K)
    row_src = jnp.zeros((p,), jnp.int32).at[slot].set(token, unique_indices=True)
    tile_start = jnp.arange(p // tm, dtype=jnp.int32) * tm
    tile_expert = jnp.minimum(jnp.searchsorted(ends, tile_start, side="right"), N_EXPERTS - 1).astype(jnp.int32)
    n_used = (ends[-1] // tm).astype(jnp.int32).reshape(1)
    return slot, row_src, tile_expert, n_used


def _deinterleave_body(w_ref, perm_ref, o_ref):
    o_ref[...] = jnp.dot(w_ref[...].astype(BF16), perm_ref[...], preferred_element_type=F32).astype(o_ref.dtype)


def _gate_up_weights(w_gate_up_all, layer):
    _, ne, d, n = w_gate_up_all.shape
    tk = 1024
    src = np.concatenate([np.arange(0, n, 2), np.arange(1, n, 2)])
    perm = np.zeros((n, n), np.float32)
    perm[src, np.arange(n)] = 1.0
    return pl.pallas_call(
        _deinterleave_body,
        out_shape=jax.ShapeDtypeStruct((ne, d, n), BF16),
        grid=(ne, d // tk),
        in_specs=[pl.BlockSpec((None, None, tk, n), lambda e, k: (layer, e, k, 0)),
                  pl.BlockSpec((n, n), lambda e, k: (0, 0))],
        out_specs=pl.BlockSpec((None, tk, n), lambda e, k: (e, k, 0)),
        compiler_params=_cparams(2),
        name="moe_gate_up_prep",
    )(w_gate_up_all, jnp.asarray(perm, BF16))


SUBLANES = 8


def _start_row_gathers(idx_of, n_rows, src3_ref, dst_of, sem):
    def group(gi, c):
        for u in range(SUBLANES):
            row = idx_of(gi * SUBLANES + u)
            pltpu.make_async_copy(src3_ref.at[row >> 3, pl.ds(row & 7, 1)], dst_of(gi, u), sem).start()
        return c

    lax.fori_loop(0, n_rows // SUBLANES, group, 0)


def _expert_body(te_ref, nused_ref, src_ref, h_ref, wgu_ref, bgu_ref, wd_ref, bd_ref, o_ref,
                 xbuf, sem, *, tm):
    i = pl.program_id(0)
    n_used = nused_ref[0]

    def gather(tile, buf):
        base = tile * tm
        _start_row_gathers(lambda r: src_ref[base + r], tm, h_ref,
                           lambda gi, u: xbuf.at[buf, gi, pl.ds(u, 1)], sem.at[buf])

    @pl.when(i == 0)
    def _():
        gather(0, 0)

    @pl.when(i + 1 < n_used)
    def _():
        gather(i + 1, (i + 1) % 2)

    @pl.when(i < n_used)
    def _():
        buf = i % 2
        pltpu.make_async_copy(h_ref.at[pl.ds(0, tm // SUBLANES)], xbuf.at[buf], sem.at[buf]).wait()
        lo, hi = _unpack_pairs(xbuf[buf].reshape(tm, HALF))
        hdn = (jnp.dot(lo.astype(BF16), wgu_ref[:HALF, :], preferred_element_type=F32)
               + jnp.dot(hi.astype(BF16), wgu_ref[HALF:, :], preferred_element_type=F32) + bgu_ref[...])
        gate = jnp.minimum(hdn[:, :D_EXPERT], SWIGLU_LIMIT)
        up = jnp.clip(hdn[:, D_EXPERT:], -SWIGLU_LIMIT, SWIGLU_LIMIT)
        act = (up + 1.0) * gate * jax.nn.sigmoid(SWIGLU_ALPHA * gate)
        y = jnp.dot(act.astype(BF16), wd_ref[...], preferred_element_type=F32) + bd_ref[...]
        o_ref[...] = _pack_pairs(y)

    @pl.when(i >= n_used)
    def _():
        o_ref[...] = jnp.zeros_like(o_ref)


def _experts(h_pk, row_src, tile_expert, n_used, wgu, bgu, wd, bd, tm):
    p = row_src.shape[0]
    t, hd = h_pk.shape
    d = 2 * hd
    return pl.pallas_call(
        functools.partial(_expert_body, tm=tm),
        out_shape=jax.ShapeDtypeStruct((p, hd), jnp.uint32),
        grid_spec=pltpu.PrefetchScalarGridSpec(
            num_scalar_prefetch=3,
            grid=(p // tm,),
            in_specs=[
                pl.BlockSpec(memory_space=pl.ANY),
                pl.BlockSpec((None, d, 2 * D_EXPERT), lambda i, te, nu, src: (te[i], 0, 0)),
                pl.BlockSpec((None, 1, 2 * D_EXPERT), lambda i, te, nu, src: (te[i], 0, 0)),
                pl.BlockSpec((None, D_EXPERT, d), lambda i, te, nu, src: (te[i], 0, 0)),
                pl.BlockSpec((None, 1, d), lambda i, te, nu, src: (te[i], 0, 0)),
            ],
            out_specs=pl.BlockSpec((tm, hd), lambda i, te, nu, src: (i, 0)),
            scratch_shapes=[pltpu.VMEM((2, tm // SUBLANES, SUBLANES, hd), jnp.uint32),
                            pltpu.SemaphoreType.DMA((2,))],
        ),
        compiler_params=_cparams(1),
        name="moe_experts",
    )(tile_expert, n_used, row_src, h_pk.reshape(t // SUBLANES, SUBLANES, hd), wgu, bgu, wd, bd)


def _combine_body(slot_ref, y_ref, gate_ref, h_ref, g_ref, b_ref, o_ref, obf_ref, ybuf, sem, *, tc):
    i = pl.program_id(0)
    n = pl.num_programs(0)

    def gather(tile, buf):
        base = tile * tc
        for k in range(TOP_K):
            _start_row_gathers(lambda r, k=k: slot_ref[(base + r) * TOP_K + k], tc, y_ref,
                               lambda gi, u, k=k: ybuf.at[buf, k, gi, pl.ds(u, 1)], sem.at[buf])

    @pl.when(i == 0)
    def _():
        gather(0, 0)

    @pl.when(i + 1 < n)
    def _():
        gather(i + 1, (i + 1) % 2)

    buf = i % 2
    for k in range(TOP_K):
        pltpu.make_async_copy(y_ref.at[pl.ds(0, tc // SUBLANES)], ybuf.at[buf, k], sem.at[buf]).wait()
    gates = gate_ref[...]
    ffn_lo = ffn_hi = None
    for k in range(TOP_K):
        lo, hi = _unpack_pairs(ybuf[buf, k].reshape(tc, HALF))
        gk = gates[:, k:k + 1]
        ffn_lo = gk * lo if ffn_lo is None else ffn_lo + gk * lo
        ffn_hi = gk * hi if ffn_hi is None else ffn_hi + gk * hi
    ffn = jnp.concatenate([ffn_lo, ffn_hi], axis=1)
    y = _layer_norm_rows(DEEPNORM_ALPHA * h_ref[...] + ffn, g_ref[...], b_ref[...])
    o_ref[...] = y
    obf_ref[...] = y.astype(BF16)


def _combine_ln(slot, y_pk, gates, h, g, b):
    t, d = h.shape
    p, hd = y_pk.shape
    tc = min(128, t)
    row = lambda i, s: (i, 0)
    vec = lambda i, s: (0, 0)
    return pl.pallas_call(
        functools.partial(_combine_body, tc=tc),
        out_shape=(jax.ShapeDtypeStruct((t, d), F32), jax.ShapeDtypeStruct((t, d), BF16)),
        grid_spec=pltpu.PrefetchScalarGridSpec(
            num_scalar_prefetch=1,
            grid=(t // tc,),
            in_specs=[
                pl.BlockSpec(memory_space=pl.ANY),
                pl.BlockSpec((tc, LANES), row),
                pl.BlockSpec((tc, d), row),
                pl.BlockSpec((1, d), vec),
                pl.BlockSpec((1, d), vec),
            ],
            out_specs=(pl.BlockSpec((tc, d), row), pl.BlockSpec((tc, d), row)),
            scratch_shapes=[pltpu.VMEM((2, TOP_K, tc // SUBLANES, SUBLANES, hd), jnp.uint32),
                            pltpu.SemaphoreType.DMA((2,))],
        ),
        compiler_params=_cparams(1),
        name="moe_combine_ln",
    )(slot, y_pk.reshape(p // SUBLANES, SUBLANES, hd), gates, h, g.reshape(1, d), b.reshape(1, d))


def _moe_ln(h, h_pk, layer, router_w, router_b, w_gate_up_all, b_gate_up, w_down_all, b_down, g, b):
    tm = MOE_TILE
    top_idx, gates = _router(h_pk, router_w, router_b)
    slot, row_src, tile_expert, n_used = _route_plan(top_idx, tm)
    wgu = _gate_up_weights(w_gate_up_all, layer)
    bgu = jnp.concatenate([b_gate_up[..., 0::2], b_gate_up[..., 1::2]], axis=-1).astype(F32)
    bgu = bgu.reshape(N_EXPERTS, 1, 2 * D_EXPERT)
    wd = _layer_bf16(w_down_all, layer, "cast_w_down").reshape(N_EXPERTS, D_EXPERT, D_MODEL)
    bd = b_down.astype(F32).reshape(N_EXPERTS, 1, D_MODEL)
    y_pk = _experts(h_pk, row_src, tile_expert, n_used, wgu, bgu, wd, bd, tm)
    return _combine_ln(slot, y_pk, gates, h, g, b)


def _even_mixer(h_bf, i, w_in_all, pool_w, pool_scale, conv_w, conv_b, dt_bias, a_log, d_skip, norm_w,
                w_out_all, batch, seq):
    w_in_bf = _layer_bf16(w_in_all, i, "cast_even_w_in")
    w_dt = jnp.pad(w_in_all[i, :, EVEN_MAIN:], ((0, 0), (0, LANES - SSM_HEADS))).astype(BF16)
    u_main = _matmul([h_bf], w_in_bf, EVEN_MAIN, 512, F32, "even_in_proj")
    dt_raw = _matmul([h_bf], w_dt, LANES, LANES, F32, "even_dt_proj")
    y_pool = _pool(u_main, pool_w.astype(BF16), pool_scale.astype(F32), seq)
    y_ssd = _ssd(u_main, dt_raw, conv_w, conv_b, dt_bias, a_log, d_skip, norm_w, batch, seq)
    w_out_bf = _layer_bf16(w_out_all, i, "cast_even_w_out")
    return _matmul([y_pool, y_ssd], w_out_bf, D_MODEL, 512, F32, "even_out_proj")


def _attn_mixer(h_bf, i, w_qkv_all, w_out_all, bias_tabs, batch, seq):
    w_qkv_bf = _layer_bf16(w_qkv_all, i, "cast_attn_w_qkv")
    qkv_heads = _matmul([h_bf], w_qkv_bf, 3 * D_MODEL, 512, F32, "attn_qkv_proj", head_out=True,
                        scaled_cols=D_MODEL, scale=ATTN_HEAD_DIM ** -0.5)
    o = _attention(qkv_heads, bias_tabs, batch, seq)
    w_out_bf = _layer_bf16(w_out_all, i, "cast_attn_w_out")
    return _matmul([o], w_out_bf, D_MODEL, 512, F32, "attn_out_proj")


def kernel(x, rel_bias, even_w_in, pool_w, pool_scale, conv_w, conv_b, dt_bias, a_log, d_skip, ssm_norm_w, even_w_out, attn_w_qkv, attn_w_out, ln1_g, ln1_b, ln2_g, ln2_b, router_w, router_b, w_gate_up, b_gate_up, w_down, b_down):
    batch, seq, d = x.shape
    t = batch * seq
    h = x.reshape(t, d).astype(F32)
    h_bf = _cast_rows_bf16(h, 0, t, "cast_x")
    bias_tabs = _attn_bias_tables(rel_bias)
    for layer in range(DEPTH):
        i = layer // 2
        if layer % 2 == 0:
            mix = _even_mixer(h_bf, i, even_w_in, pool_w[i], pool_scale[i], conv_w[i], conv_b[i],
                              dt_bias[i], a_log[i], d_skip[i], ssm_norm_w[i], even_w_out, batch, seq)
        else:
            mix = _attn_mixer(h_bf, i, attn_w_qkv, attn_w_out, bias_tabs, batch, seq)
        h, h_pk = _residual_ln(h, mix, ln1_g[layer], ln1_b[layer])
        h, h_bf = _moe_ln(h, h_pk, layer, router_w[layer], router_b[layer], w_gate_up, b_gate_up[layer],
                          w_down, b_down[layer], ln2_g[layer], ln2_b[layer])
    return h.reshape(batch, seq, d).astype(x.dtype)
```

```python
import functools
import math

import numpy as np
import jax
import jax.numpy as jnp
from jax import lax
from jax.experimental import pallas as pl
from jax.experimental.pallas import tpu as pltpu

F32 = jnp.float32
BF16 = jnp.bfloat16

D_MODEL = 4096
DEPTH = 4
POOL_WINDOWS = (2, 4, 8, 16)
POOL_GROUP = D_MODEL // 8
POOL_WIDTH = POOL_GROUP * len(POOL_WINDOWS)
SSM_HEAD_DIM = 64
SSM_INNER = D_MODEL // 2
SSM_HEADS = SSM_INNER // SSM_HEAD_DIM
SSM_GROUPS = 4
SSM_STATE = 128
SSM_CONV = 4
SSM_CHUNK = 128
SSM_CONV_DIM = SSM_INNER + 2 * SSM_GROUPS * SSM_STATE
EVEN_MAIN = POOL_WIDTH + SSM_INNER + SSM_CONV_DIM
ATTN_HEAD_DIM = 128
ATTN_HEADS = D_MODEL // ATTN_HEAD_DIM
DILATED = ((128, 1), (512, 4), (2048, 16))
ATTN_BLOCK = 128
ATTN_SUPER = ATTN_BLOCK * 16
REL_BUCKETS = 32
REL_MAX_DIST = 2048
N_EXPERTS = 32
TOP_K = 4
D_EXPERT = 256
SWIGLU_LIMIT = 7.0
SWIGLU_ALPHA = 1.702
DEEPNORM_ALPHA = (2 * DEPTH) ** 0.25
LN_EPS = 1e-5
RMS_EPS = 1e-5

LANES = 128
NEG = -1e30
VMEM_LIMIT = 56 * 1024 * 1024
MOE_TILE = 256


def _cparams(n_axes):
    return pltpu.CompilerParams(dimension_semantics=("arbitrary",) * n_axes,
                                vmem_limit_bytes=VMEM_LIMIT)


def _mm_body(*refs, n_lhs, k_bounds, head_out, scaled_blocks, scale):
    lhs = refs[:n_lhs]
    w_ref = refs[n_lhs]
    o_ref = refs[n_lhs + 1]
    acc = None
    for l_ref, (k0, k1) in zip(lhs, k_bounds):
        part = jnp.dot(l_ref[...], w_ref[k0:k1, :], preferred_element_type=F32)
        acc = part if acc is None else acc + part
    if scaled_blocks:
        acc = acc * jnp.where(pl.program_id(1) < scaled_blocks, scale, 1.0)
    if head_out:
        for j in range(o_ref.shape[0]):
            o_ref[j] = acc[:, j * LANES:(j + 1) * LANES].astype(o_ref.dtype)
    else:
        o_ref[...] = acc.astype(o_ref.dtype)


def _matmul(lhs_list, w, n_out, tn, out_dtype, name, head_out=False, scaled_cols=0, scale=1.0):
    assert scaled_cols % tn == 0
    m = lhs_list[0].shape[0]
    k = w.shape[0]
    tm = min(1024, m)
    k_bounds = []
    k0 = 0
    for l in lhs_list:
        k_bounds.append((k0, k0 + l.shape[1]))
        k0 += l.shape[1]
    assert k0 == k and m % tm == 0 and n_out % tn == 0
    in_specs = [pl.BlockSpec((tm, l.shape[1]), lambda i, j: (i, 0)) for l in lhs_list]
    in_specs.append(pl.BlockSpec((k, tn), lambda i, j: (0, j)))
    if head_out:
        out_shape = jax.ShapeDtypeStruct((n_out // LANES, m, LANES), out_dtype)
        out_spec = pl.BlockSpec((tn // LANES, tm, LANES), lambda i, j: (j, i, 0))
    else:
        out_shape = jax.ShapeDtypeStruct((m, n_out), out_dtype)
        out_spec = pl.BlockSpec((tm, tn), lambda i, j: (i, j))
    return pl.pallas_call(
        functools.partial(_mm_body, n_lhs=len(lhs_list), k_bounds=tuple(k_bounds), head_out=head_out,
                          scaled_blocks=scaled_cols // tn, scale=scale),
        out_shape=out_shape,
        grid=(m // tm, n_out // tn),
        in_specs=in_specs,
        out_specs=out_spec,
        compiler_params=_cparams(2),
        name=name,
    )(*lhs_list, w)


def _cast_body(w_ref, o_ref):
    o_ref[...] = w_ref[...].astype(o_ref.dtype)


def _cast_rows_bf16(w2d, row0, nrows, name):
    n = w2d.shape[1]
    tk = 256
    assert row0 % tk == 0 and nrows % tk == 0
    return pl.pallas_call(
        _cast_body,
        out_shape=jax.ShapeDtypeStruct((nrows, n), BF16),
        grid=(nrows // tk,),
        in_specs=[pl.BlockSpec((tk, n), lambda i: (row0 // tk + i, 0))],
        out_specs=pl.BlockSpec((tk, n), lambda i: (i, 0)),
        compiler_params=_cparams(1),
        name=name,
    )(w2d)


def _layer_bf16(w_stacked, layer, name):
    n = w_stacked.shape[-1]
    rows = math.prod(w_stacked.shape[1:-1])
    return _cast_rows_bf16(w_stacked.reshape(-1, n), layer * rows, rows, name)


HALF = D_MODEL // 2


def _pack_pairs(y):
    lo = lax.bitcast_convert_type(y[:, :HALF].astype(BF16).astype(F32), jnp.uint32)
    hi = lax.bitcast_convert_type(y[:, HALF:].astype(BF16).astype(F32), jnp.uint32)
    return hi | (lo >> 16)


def _unpack_pairs(w):
    lo = lax.bitcast_convert_type(w << 16, F32)
    hi = lax.bitcast_convert_type(w & jnp.uint32(0xFFFF0000), F32)
    return lo, hi


def _layer_norm_rows(x, g, b):
    mu = jnp.mean(x, axis=-1, keepdims=True)
    xc = x - mu
    var = jnp.mean(xc * xc, axis=-1, keepdims=True)
    return xc * lax.rsqrt(var + LN_EPS) * g + b


def _ln_body(h_ref, mix_ref, g_ref, b_ref, o_ref, opk_ref):
    y = _layer_norm_rows(DEEPNORM_ALPHA * h_ref[...] + mix_ref[...], g_ref[...], b_ref[...])
    o_ref[...] = y
    opk_ref[...] = _pack_pairs(y)


def _residual_ln(h, mix, g, b):
    t, d = h.shape
    tm = min(256, t)
    row = pl.BlockSpec((tm, d), lambda i: (i, 0))
    vec = pl.BlockSpec((1, d), lambda i: (0, 0))
    return pl.pallas_call(
        _ln_body,
        out_shape=(jax.ShapeDtypeStruct((t, d), F32), jax.ShapeDtypeStruct((t, HALF), jnp.uint32)),
        grid=(t // tm,),
        in_specs=[row, row, vec, vec],
        out_specs=(row, pl.BlockSpec((tm, HALF), lambda i: (i, 0))),
        compiler_params=_cparams(1),
        name="residual_ln",
    )(h, mix, g.reshape(1, d), b.reshape(1, d))


POOL_HALO = 16


def _pool_body(u_ref, halo_ref, w_ref, scale_ref, o_ref, buf, *, ts, tiles_per_seq):
    i = pl.program_id(0)
    first = (i % tiles_per_seq) == 0
    buf[0:POOL_HALO, :] = jnp.where(first, 0.0, halo_ref[...])
    buf[POOL_HALO:POOL_HALO + ts, :] = u_ref[...]
    pos = (i % tiles_per_seq) * ts + lax.broadcasted_iota(jnp.int32, (ts, 1), 0)
    for g, w in enumerate(POOL_WINDOWS):
        c0, c1 = g * POOL_GROUP, (g + 1) * POOL_GROUP
        cur = buf[POOL_HALO:POOL_HALO + ts, c0:c1]
        acc = cur
        for j in range(1, w):
            acc = acc + buf[POOL_HALO - j:POOL_HALO - j + ts, c0:c1]
        count = jnp.minimum(pos + 1, w).astype(F32)
        d = acc / count - cur
        y = jnp.dot(d.astype(BF16), w_ref[g], preferred_element_type=F32)
        o_ref[:, c0:c1] = (y * scale_ref[:, c0:c1]).astype(o_ref.dtype)


def _pool(u_main, pool_w_bf, pool_scale, seq):
    t = u_main.shape[0]
    ts = min(512, seq)
    tiles_per_seq = seq // ts
    hb = ts // POOL_HALO
    return pl.pallas_call(
        functools.partial(_pool_body, ts=ts, tiles_per_seq=tiles_per_seq),
        out_shape=jax.ShapeDtypeStruct((t, POOL_WIDTH), BF16),
        grid=(t // ts,),
        in_specs=[
            pl.BlockSpec((ts, POOL_WIDTH), lambda i: (i, 0)),
            pl.BlockSpec((POOL_HALO, POOL_WIDTH), lambda i: (jnp.maximum(i * hb - 1, 0), 0)),
            pl.BlockSpec((len(POOL_WINDOWS), POOL_GROUP, POOL_GROUP), lambda i: (0, 0, 0)),
            pl.BlockSpec((1, POOL_WIDTH), lambda i: (0, 0)),
        ],
        out_specs=pl.BlockSpec((ts, POOL_WIDTH), lambda i: (i, 0)),
        scratch_shapes=[pltpu.VMEM((ts + POOL_HALO, POOL_WIDTH), F32)],
        compiler_params=_cparams(1),
        name="pool_mixer",
    )(u_main, u_main, pool_w_bf, pool_scale.reshape(1, POOL_WIDTH))


CONV_HALO = 8
PAIR = 2 * SSM_HEAD_DIM
N_PAIRS = SSM_HEADS // 2
PAIRS_PER_GROUP = N_PAIRS // SSM_GROUPS


def _softplus(x):
    return jnp.maximum(x, 0.0) + jnp.log1p(jnp.exp(-jnp.abs(x)))


def _silu(x):
    return x * jax.nn.sigmoid(x)


def _ssd_body(z_ref, xs_ref, bc_ref, dt_ref, convw_ref, convb_ref, dtb_row_ref, dtb_col_ref,
              alog_row_ref, alog_col_ref, dskip_ref, normw_ref, o_ref,
              buf, xbc, carry, state, ybuf):
    L = SSM_CHUNK
    c = pl.program_id(1)

    @pl.when(c == 0)
    def _():
        carry[...] = jnp.zeros_like(carry)
        state[...] = jnp.zeros_like(state)

    buf[0:CONV_HALO, :] = carry[...]
    buf[CONV_HALO:CONV_HALO + L, 0:SSM_INNER] = xs_ref[...]
    buf[CONV_HALO:CONV_HALO + L, SSM_INNER:SSM_CONV_DIM] = bc_ref[...]
    carry[...] = buf[L:L + CONV_HALO, :]
    cw = 512
    for j in range(SSM_CONV_DIM // cw):
        cs = slice(j * cw, (j + 1) * cw)
        acc = convb_ref[:, cs] + buf[CONV_HALO - 3:CONV_HALO - 3 + L, cs] * convw_ref[0:1, cs]
        for i in range(1, SSM_CONV):
            off = CONV_HALO - 3 + i
            acc = acc + buf[off:off + L, cs] * convw_ref[i:i + 1, cs]
        xbc[:, cs] = _silu(acc)

    ri = lax.broadcasted_iota(jnp.int32, (L, L), 0)
    ci = lax.broadcasted_iota(jnp.int32, (L, L), 1)
    tril = ri >= ci
    dt_blk = dt_ref[...]
    dt = _softplus(dt_blk + dtb_row_ref[...])
    da = dt * (-jnp.exp(alog_row_ref[...]))
    acum = jnp.dot(tril.astype(F32), da, precision=lax.Precision.HIGHEST,
                   preferred_element_type=F32)
    dt_t = _softplus(dt_blk.T + dtb_col_ref[...])
    da_t = dt_t * (-jnp.exp(alog_col_ref[...]))
    acum_t = jnp.dot(da_t, (ri <= ci).astype(F32), precision=lax.Precision.HIGHEST,
                     preferred_element_type=F32)
    w_t = jnp.exp(acum_t[:, L - 1:L] - acum_t) * dt_t
    eac = jnp.exp(acum)
    cd_row = jnp.exp(acum[L - 1:L, :])

    lane = lax.broadcasted_iota(jnp.int32, (L, PAIR), 1)
    lane_row = lax.broadcasted_iota(jnp.int32, (1, PAIR), 1)
    nt = (((1,), (1,)), ((), ()))
    for g in range(SSM_GROUPS):
        b0 = SSM_INNER + g * SSM_STATE
        c0 = SSM_INNER + SSM_GROUPS * SSM_STATE + g * SSM_STATE
        bm = xbc[:, b0:b0 + SSM_STATE]
        cm = xbc[:, c0:c0 + SSM_STATE]
        cb = lax.dot_general(cm.astype(BF16), bm.astype(BF16), nt, preferred_element_type=F32)
        bm_t = bm.T
        for j in range(PAIRS_PER_GROUP):
            p = g * PAIRS_PER_GROUP + j
            xs_pair = xbc[:, p * PAIR:(p + 1) * PAIR].astype(BF16)
            st = state[p]
            lhs_y, lhs_s = [], []
            for h in (2 * p, 2 * p + 1):
                seg = acum[:, h:h + 1] - acum_t[h:h + 1, :]
                dec = jnp.exp(jnp.where(tril, seg, -jnp.inf))
                m_h = cb * dec * dt_t[h:h + 1, :]
                e_h = eac[:, h:h + 1] * cm
                lhs_y.append(jnp.concatenate([m_h, e_h], axis=1).astype(BF16))
                lhs_s.append((bm_t * w_t[h:h + 1, :]).astype(BF16))
            rhs = jnp.concatenate([xs_pair, st.astype(BF16)], axis=0)
            r = jnp.dot(jnp.concatenate(lhs_y, axis=0), rhs, preferred_element_type=F32)
            ybuf[:, p * PAIR:(p + 1) * PAIR] = jnp.where(lane < SSM_HEAD_DIM, r[0:L], r[L:2 * L])
            s_new = jnp.dot(jnp.concatenate(lhs_s, axis=0), xs_pair, preferred_element_type=F32)
            contrib = jnp.where(lane < SSM_HEAD_DIM, s_new[0:L], s_new[L:2 * L])
            cdp = jnp.where(lane_row < SSM_HEAD_DIM, cd_row[:, 2 * p:2 * p + 1], cd_row[:, 2 * p + 1:2 * p + 2])
            state[p] = st * cdp + contrib

    gw = SSM_INNER // SSM_GROUPS
    for g in range(SSM_GROUPS):
        cs = slice(g * gw, (g + 1) * gw)
        y = ybuf[:, cs] + xbc[:, cs] * dskip_ref[:, cs]
        y = y * _silu(z_ref[:, cs])
        ms = jnp.mean(y * y, axis=-1, keepdims=True)
        o_ref[:, cs] = (y * lax.rsqrt(ms + RMS_EPS) * normw_ref[:, cs]).astype(o_ref.dtype)


def _ssd(u_main, dt_raw, conv_w, conv_b, dt_bias, a_log, d_skip, norm_w, batch, seq):
    t = u_main.shape[0]
    L = SSM_CHUNK
    nc = seq // L
    pad = LANES - SSM_HEADS
    dtb = jnp.pad(dt_bias.astype(F32), (0, pad))
    alog = jnp.pad(a_log.astype(F32), (0, pad))
    dskip = jnp.repeat(d_skip.astype(F32), SSM_HEAD_DIM).reshape(1, SSM_INNER)
    zcol = POOL_WIDTH // SSM_INNER
    xcol = (POOL_WIDTH + SSM_INNER) // SSM_INNER
    bc_w = 2 * SSM_GROUPS * SSM_STATE
    bcol = (POOL_WIDTH + 2 * SSM_INNER) // bc_w
    rowmap = lambda col: (lambda b, c: (b * nc + c, col))
    full = lambda shape: pl.BlockSpec(shape, lambda b, c: (0,) * len(shape))
    return pl.pallas_call(
        _ssd_body,
        out_shape=jax.ShapeDtypeStruct((t, SSM_INNER), BF16),
        grid=(batch, nc),
        in_specs=[
            pl.BlockSpec((L, SSM_INNER), rowmap(zcol)),
            pl.BlockSpec((L, SSM_INNER), rowmap(xcol)),
            pl.BlockSpec((L, bc_w), rowmap(bcol)),
            pl.BlockSpec((L, LANES), rowmap(0)),
            full((SSM_CONV, SSM_CONV_DIM)),
            full((1, SSM_CONV_DIM)),
            full((1, LANES)),
            full((LANES, 1)),
            full((1, LANES)),
            full((LANES, 1)),
            full((1, SSM_INNER)),
            full((1, SSM_INNER)),
        ],
        out_specs=pl.BlockSpec((L, SSM_INNER), rowmap(0)),
        scratch_shapes=[
            pltpu.VMEM((L + CONV_HALO, SSM_CONV_DIM), F32),
            pltpu.VMEM((L, SSM_CONV_DIM), F32),
            pltpu.VMEM((CONV_HALO, SSM_CONV_DIM), F32),
            pltpu.VMEM((N_PAIRS, SSM_STATE, PAIR), F32),
            pltpu.VMEM((L, SSM_INNER), F32),
        ],
        compiler_params=_cparams(2),
        name="ssd_mixer",
    )(u_main, u_main, u_main, dt_raw, conv_w.astype(F32), conv_b.reshape(1, SSM_CONV_DIM).astype(F32),
      dtb.reshape(1, LANES), dtb.reshape(LANES, 1), alog.reshape(1, LANES), alog.reshape(LANES, 1),
      dskip, norm_w.reshape(1, SSM_INNER).astype(F32))


def _t5_bucket(dist):
    max_exact = REL_BUCKETS // 2
    d = np.maximum(dist, 1).astype(np.float32)
    large = max_exact + (np.log(d / max_exact) / np.log(REL_MAX_DIST / max_exact)
                         * (REL_BUCKETS - max_exact)).astype(np.int32)
    large = np.minimum(large, REL_BUCKETS - 1)
    return np.where(dist < max_exact, dist, large).astype(np.int32)


def _attn_bias_tables(rel_bias):
    blk = ATTN_BLOCK
    delta = blk + np.arange(blk)[:, None] - np.arange(2 * blk)[None, :]
    tabs = []
    for window, dil in DILATED:
        n_back = window // dil
        assert n_back <= blk
        band = (delta >= 0) & (delta <= n_back)
        rel = rel_bias[_t5_bucket(np.clip(delta, 0, n_back) * dil)]
        rel = jnp.transpose(rel, (2, 0, 1)).astype(F32)
        tabs.append(jnp.where(band[None], rel, NEG))
    return jnp.stack(tabs, 0)


ATTN_GROUP = 16


def _attn_body(q_ref, kp_ref, kc_ref, vp_ref, vc_ref, bias_ref, o_ref, m_ref, l_ref, acc_ref):
    blk = ATTN_BLOCK
    grp = ATTN_GROUP
    seq_first = pl.program_id(2) == 0
    nt = (((1,), (1,)), ((), ()))
    col = lax.broadcasted_iota(jnp.int32, (blk, 2 * blk), 1)

    def rows(start, n, d):
        if d > 1:
            return pl.ds(start, n, stride=d)
        return pl.ds(start if isinstance(start, int) else pl.multiple_of(start, blk), n)

    for bi, (_, d) in enumerate(DILATED):
        span = blk * d
        nblk = ATTN_SUPER // span
        bias = bias_ref[bi]
        bias_first = jnp.where(jnp.logical_and(seq_first, col < blk), NEG, bias)

        def block(qs, first, bi=bi, d=d, span=span, bias=bias, bias_first=bias_first):
            qr = rows(qs, blk, d)
            q = q_ref[qr, :].astype(BF16)
            if first:
                pr = rows(ATTN_SUPER - span + qs, blk, d)
                k = jnp.concatenate([kp_ref[pr, :], kc_ref[qr, :]], axis=0)
                v = jnp.concatenate([vp_ref[pr, :], vc_ref[qr, :]], axis=0)
                b = bias_first
            else:
                wr = rows(qs - span, 2 * blk, d)
                k = kc_ref[wr, :]
                v = vc_ref[wr, :]
                b = bias
            s = lax.dot_general(q, k.astype(BF16), nt, preferred_element_type=F32) + b
            rm = jnp.max(s, axis=1, keepdims=True)
            p = jnp.exp(s - rm)
            rs = jnp.sum(p, axis=1, keepdims=True)
            pv = jnp.dot(p.astype(BF16), v.astype(BF16), preferred_element_type=F32)
            m_ref[bi, qr, :] = jnp.broadcast_to(rm, (blk, LANES))
            l_ref[bi, qr, :] = jnp.broadcast_to(rs, (blk, LANES))
            acc_ref[bi, qr, :] = pv

        if nblk == 1:
            def body(i, c, block=block):
                for u in range(grp):
                    block(i * grp + u, True)
                return c
            lax.fori_loop(0, d // grp, body, 0)
        elif d == 1:
            block(0, True)
            for u in range(1, grp):
                block(u * span, False)

            def body(i, c, block=block, span=span):
                for u in range(grp):
                    block((i * grp + u) * span, False)
                return c
            lax.fori_loop(1, nblk // grp, body, 0)
        else:
            rpi = max(grp // nblk, 1)

            def body(i, c, block=block, span=span, nblk=nblk, rpi=rpi):
                for u in range(rpi):
                    r = i * rpi + u
                    block(r, True)
                    for n in range(1, nblk):
                        block(n * span + r, False)
                return c
            lax.fori_loop(0, d // rpi, body, 0)

    def merge(c, carry):
        rr = pl.ds(pl.multiple_of(c * blk, blk), blk)
        ms = [m_ref[i, rr, :] for i in range(len(DILATED))]
        mm = functools.reduce(jnp.maximum, ms)
        ws = [jnp.exp(m - mm) for m in ms]
        den = functools.reduce(lambda a, b: a + b, [w * l_ref[i, rr, :] for i, w in enumerate(ws)])
        num = functools.reduce(lambda a, b: a + b, [w * acc_ref[i, rr, :] for i, w in enumerate(ws)])
        o_ref[rr, :] = (num / den).astype(o_ref.dtype)
        return carry

    lax.fori_loop(0, ATTN_SUPER // blk, merge, 0)


def _attention(qkv_heads, bias_tabs, batch, seq):
    t = qkv_heads.shape[1]
    hn = ATTN_HEADS
    sup = ATTN_SUPER
    nst = seq // sup
    assert seq % sup == 0
    cur = lambda off: (lambda b, h, s: (off + h, b * nst + s, 0))
    prev = lambda off: (lambda b, h, s: (off + h, b * nst + jnp.maximum(s - 1, 0), 0))
    slab = lambda imap: pl.BlockSpec((None, sup, LANES), imap)
    return pl.pallas_call(
        _attn_body,
        out_shape=jax.ShapeDtypeStruct((t, D_MODEL), BF16),
        grid=(batch, hn, nst),
        in_specs=[
            slab(cur(0)),
            slab(prev(hn)), slab(cur(hn)),
            slab(prev(2 * hn)), slab(cur(2 * hn)),
            pl.BlockSpec((len(DILATED), None, ATTN_BLOCK, 2 * ATTN_BLOCK), lambda b, h, s: (0, h, 0, 0)),
        ],
        out_specs=pl.BlockSpec((sup, LANES), lambda b, h, s: (b * nst + s, h)),
        scratch_shapes=[pltpu.VMEM((len(DILATED), sup, LANES), F32)] * 3,
        compiler_params=_cparams(3),
        name="dilated_attention",
    )(qkv_heads, qkv_heads, qkv_heads, qkv_heads, qkv_heads, bias_tabs)


def _router_body(h_ref, w_ref, b_ref, idx_ref, gate_ref):
    lo, hi = _unpack_pairs(h_ref[...])
    logits = (jnp.dot(lo.astype(BF16), w_ref[:HALF, :], preferred_element_type=F32)
              + jnp.dot(hi.astype(BF16), w_ref[HALF:, :], preferred_element_type=F32) + b_ref[...])
    lane = lax.broadcasted_iota(jnp.int32, logits.shape, 1)
    vals, idxs = [], []
    for _ in range(TOP_K):
        m = jnp.max(logits, axis=1, keepdims=True)
        idx = jnp.min(jnp.where(logits == m, lane, LANES), axis=1, keepdims=True)
        vals.append(m)
        idxs.append(idx)
        logits = jnp.where(lane == idx, -jnp.inf, logits)
    es = [jnp.exp(v - vals[0]) for v in vals]
    den = es[0]
    for e in es[1:]:
        den = den + e
    idx_out = jnp.zeros(logits.shape, jnp.int32)
    gate_out = jnp.zeros(logits.shape, F32)
    for k in range(TOP_K):
        idx_out = jnp.where(lane == k, idxs[k], idx_out)
        gate_out = jnp.where(lane == k, es[k] / den, gate_out)
    idx_ref[...] = idx_out
    gate_ref[...] = gate_out


def _router(h_pk, router_w, router_b):
    t, hd = h_pk.shape
    d = 2 * hd
    tm = min(1024, t)
    pad = LANES - N_EXPERTS
    w = jnp.pad(router_w, ((0, 0), (0, pad))).astype(BF16)
    b = jnp.pad(router_b.astype(F32), (0, pad), constant_values=NEG).reshape(1, LANES)
    idx, gates = pl.pallas_call(
        _router_body,
        out_shape=(jax.ShapeDtypeStruct((t, LANES), jnp.int32), jax.ShapeDtypeStruct((t, LANES), F32)),
        grid=(t // tm,),
        in_specs=[pl.BlockSpec((tm, hd), lambda i: (i, 0)),
                  pl.BlockSpec((d, LANES), lambda i: (0, 0)),
                  pl.BlockSpec((1, LANES), lambda i: (0, 0))],
        out_specs=(pl.BlockSpec((tm, LANES), lambda i: (i, 0)), pl.BlockSpec((tm, LANES), lambda i: (i, 0))),
        compiler_params=_cparams(1),
        name="moe_router",
    )(h_pk, w, b)
    return idx, gates


def _rank_body(idx_ref, rank_ref, counts_ref, carry):
    @pl.when(pl.program_id(0) == 0)
    def _():
        carry[...] = jnp.zeros_like(carry)

    idx = idx_ref[...]
    tt = idx.shape[0]
    lane = lax.broadcasted_iota(jnp.int32, idx.shape, 1)
    hits = [idx[:, k:k + 1] == lane for k in range(TOP_K)]
    per_expert = functools.reduce(lambda a, b: a + b, [h.astype(F32) for h in hits])
    earlier = (lax.broadcasted_iota(jnp.int32, (tt, tt), 0) > lax.broadcasted_iota(jnp.int32, (tt, tt), 1))
    before = jnp.dot(earlier.astype(BF16), per_expert.astype(BF16), preferred_element_type=F32) + carry[...]
    out = jnp.zeros(idx.shape, jnp.int32)
    for k in range(TOP_K):
        rk = jnp.sum(jnp.where(hits[k], before, 0.0), axis=1, keepdims=True).astype(jnp.int32)
        out = jnp.where(lane == k, rk, out)
    rank_ref[...] = out
    carry[...] = carry[...] + jnp.sum(per_expert, axis=0, keepdims=True)
    counts_ref[...] = carry[...].astype(jnp.int32)


def _slot_body(idx_ref, rank_ref, starts_ref, slot_ref):
    idx = idx_ref[...]
    lane = lax.broadcasted_iota(jnp.int32, idx.shape, 1)
    starts = starts_ref[...].astype(F32)
    out = jnp.zeros(idx.shape, jnp.int32)
    for k in range(TOP_K):
        st = jnp.sum(jnp.where(idx[:, k:k + 1] == lane, starts, 0.0), axis=1, keepdims=True)
        out = jnp.where(lane == k, st.astype(jnp.int32), out)
    slot_ref[...] = out + rank_ref[...]


def _route_plan(top_idx, tm):
    t = top_idx.shape[0]
    a = t * TOP_K
    tt = min(512, t)
    blk = pl.BlockSpec((tt, LANES), lambda i: (i, 0))
    vec = pl.BlockSpec((1, LANES), lambda i: (0, 0))
    rank, counts = pl.pallas_call(
        _rank_body,
        out_shape=(jax.ShapeDtypeStruct((t, LANES), jnp.int32), jax.ShapeDtypeStruct((1, LANES), jnp.int32)),
        grid=(t // tt,),
        in_specs=[blk],
        out_specs=(blk, vec),
        scratch_shapes=[pltpu.VMEM((1, LANES), F32)],
        compiler_params=_cparams(1),
        name="moe_rank",
    )(top_idx)
    counts = counts[0, :N_EXPERTS]
    padded = ((counts + tm - 1) // tm) * tm
    ends = jnp.cumsum(padded)
    starts = ends - padded
    slot = pl.pallas_call(
        _slot_body,
        out_shape=jax.ShapeDtypeStruct((t, LANES), jnp.int32),
        grid=(t // tt,),
        in_specs=[blk, blk, vec],
        out_specs=blk,
        compiler_params=_cparams(1),
        name="moe_slot",
    )(top_idx, rank, jnp.pad(starts, (0, LANES - N_EXPERTS)).reshape(1, LANES).astype(jnp.int32))
    slot = slot[:, :TOP_K].reshape(a)
    p = a + N_EXPERTS * tm
    token = (jnp.arange(a, dtype=jnp.int32) // TOP_K)
    row_src = jnp.zeros((p,), jnp.int32).at[slot].set(token, unique_indices=True)
    tile_start = jnp.arange(p // tm, dtype=jnp.int32) * tm
    tile_expert = jnp.minimum(jnp.searchsorted(ends, tile_start, side="right"), N_EXPERTS - 1).astype(jnp.int32)
    n_used = (ends[-1] // tm).astype(jnp.int32).reshape(1)
    return slot, row_src, tile_expert, n_used


def _deinterleave_body(w_ref, perm_ref, o_ref):
    o_ref[...] = jnp.dot(w_ref[...].astype(BF16), perm_ref[...], preferred_element_type=F32).astype(o_ref.dtype)


def _gate_up_weights(w_gate_up_all, layer):
    _, ne, d, n = w_gate_up_all.shape
    tk = 1024
    src = np.concatenate([np.arange(0, n, 2), np.arange(1, n, 2)])
    perm = np.zeros((n, n), np.float32)
    perm[src, np.arange(n)] = 1.0
    return pl.pallas_call(
        _deinterleave_body,
        out_shape=jax.ShapeDtypeStruct((ne, d, n), BF16),
        grid=(ne, d // tk),
        in_specs=[pl.BlockSpec((None, None, tk, n), lambda e, k: (layer, e, k, 0)),
                  pl.BlockSpec((n, n), lambda e, k: (0, 0))],
        out_specs=pl.BlockSpec((None, tk, n), lambda e, k: (e, k, 0)),
        compiler_params=_cparams(2),
        name="moe_gate_up_prep",
    )(w_gate_up_all, jnp.asarray(perm, BF16))


SUBLANES = 8


def _start_row_gathers(idx_of, n_rows, src3_ref, dst_of, sem):
    def group(gi, c):
        for u in range(SUBLANES):
            row = idx_of(gi * SUBLANES + u)
            pltpu.make_async_copy(src3_ref.at[row >> 3, pl.ds(row & 7, 1)], dst_of(gi, u), sem).start()
        return c

    lax.fori_loop(0, n_rows // SUBLANES, group, 0)


def _expert_body(te_ref, nused_ref, src_ref, h_ref, wgu_ref, bgu_ref, wd_ref, bd_ref, o_ref,
                 xbuf, sem, *, tm):
    i = pl.program_id(0)
    n_used = nused_ref[0]

    def gather(tile, buf):
        base = tile * tm
        _start_row_gathers(lambda r: src_ref[base + r], tm, h_ref,
                           lambda gi, u: xbuf.at[buf, gi, pl.ds(u, 1)], sem.at[buf])

    @pl.when(i == 0)
    def _():
        gather(0, 0)

    @pl.when(i + 1 < n_used)
    def _():
        gather(i + 1, (i + 1) % 2)

    @pl.when(i < n_used)
    def _():
        buf = i % 2
        pltpu.make_async_copy(h_ref.at[pl.ds(0, tm // SUBLANES)], xbuf.at[buf], sem.at[buf]).wait()
        lo, hi = _unpack_pairs(xbuf[buf].reshape(tm, HALF))
        hdn = (jnp.dot(lo.astype(BF16), wgu_ref[:HALF, :], preferred_element_type=F32)
               + jnp.dot(hi.astype(BF16), wgu_ref[HALF:, :], preferred_element_type=F32) + bgu_ref[...])
        gate = jnp.minimum(hdn[:, :D_EXPERT], SWIGLU_LIMIT)
        up = jnp.clip(hdn[:, D_EXPERT:], -SWIGLU_LIMIT, SWIGLU_LIMIT)
        act = (up + 1.0) * gate * jax.nn.sigmoid(SWIGLU_ALPHA * gate)
        y = jnp.dot(act.astype(BF16), wd_ref[...], preferred_element_type=F32) + bd_ref[...]
        o_ref[...] = _pack_pairs(y)

    @pl.when(i >= n_used)
    def _():
        o_ref[...] = jnp.zeros_like(o_ref)


def _experts(h_pk, row_src, tile_expert, n_used, wgu, bgu, wd, bd, tm):
    p = row_src.shape[0]
    t, hd = h_pk.shape
    d = 2 * hd
    return pl.pallas_call(
        functools.partial(_expert_body, tm=tm),
        out_shape=jax.ShapeDtypeStruct((p, hd), jnp.uint32),
        grid_spec=pltpu.PrefetchScalarGridSpec(
            num_scalar_prefetch=3,
            grid=(p // tm,),
            in_specs=[
                pl.BlockSpec(memory_space=pl.ANY),
                pl.BlockSpec((None, d, 2 * D_EXPERT), lambda i, te, nu, src: (te[i], 0, 0)),
                pl.BlockSpec((None, 1, 2 * D_EXPERT), lambda i, te, nu, src: (te[i], 0, 0)),
                pl.BlockSpec((None, D_EXPERT, d), lambda i, te, nu, src: (te[i], 0, 0)),
                pl.BlockSpec((None, 1, d), lambda i, te, nu, src: (te[i], 0, 0)),
            ],
            out_specs=pl.BlockSpec((tm, hd), lambda i, te, nu, src: (i, 0)),
            scratch_shapes=[pltpu.VMEM((2, tm // SUBLANES, SUBLANES, hd), jnp.uint32),
                            pltpu.SemaphoreType.DMA((2,))],
        ),
        compiler_params=_cparams(1),
        name="moe_experts",
    )(tile_expert, n_used, row_src, h_pk.reshape(t // SUBLANES, SUBLANES, hd), wgu, bgu, wd, bd)


def _combine_body(slot_ref, y_ref, gate_ref, h_ref, g_ref, b_ref, o_ref, obf_ref, ybuf, sem, *, tc):
    i = pl.program_id(0)
    n = pl.num_programs(0)

    def gather(tile, buf):
        base = tile * tc
        for k in range(TOP_K):
            _start_row_gathers(lambda r, k=k: slot_ref[(base + r) * TOP_K + k], tc, y_ref,
                               lambda gi, u, k=k: ybuf.at[buf, k, gi, pl.ds(u, 1)], sem.at[buf])

    @pl.when(i == 0)
    def _():
        gather(0, 0)

    @pl.when(i + 1 < n)
    def _():
        gather(i + 1, (i + 1) % 2)

    buf = i % 2
    for k in range(TOP_K):
        pltpu.make_async_copy(y_ref.at[pl.ds(0, tc // SUBLANES)], ybuf.at[buf, k], sem.at[buf]).wait()
    gates = gate_ref[...]
    ffn_lo = ffn_hi = None
    for k in range(TOP_K):
        lo, hi = _unpack_pairs(ybuf[buf, k].reshape(tc, HALF))
        gk = gates[:, k:k + 1]
        ffn_lo = gk * lo if ffn_lo is None else ffn_lo + gk * lo
        ffn_hi = gk * hi if ffn_hi is None else ffn_hi + gk * hi
    ffn = jnp.concatenate([ffn_lo, ffn_hi], axis=1)
    y = _layer_norm_rows(DEEPNORM_ALPHA * h_ref[...] + ffn, g_ref[...], b_ref[...])
    o_ref[...] = y
    obf_ref[...] = y.astype(BF16)


def _combine_ln(slot, y_pk, gates, h, g, b):
    t, d = h.shape
    p, hd = y_pk.shape
    tc = min(128, t)
    row = lambda i, s: (i, 0)
    vec = lambda i, s: (0, 0)
    return pl.pallas_call(
        functools.partial(_combine_body, tc=tc),
        out_shape=(jax.ShapeDtypeStruct((t, d), F32), jax.ShapeDtypeStruct((t, d), BF16)),
        grid_spec=pltpu.PrefetchScalarGridSpec(
            num_scalar_prefetch=1,
            grid=(t // tc,),
            in_specs=[
                pl.BlockSpec(memory_space=pl.ANY),
                pl.BlockSpec((tc, LANES), row),
                pl.BlockSpec((tc, d), row),
                pl.BlockSpec((1, d), vec),
                pl.BlockSpec((1, d), vec),
            ],
            out_specs=(pl.BlockSpec((tc, d), row), pl.BlockSpec((tc, d), row)),
            scratch_shapes=[pltpu.VMEM((2, TOP_K, tc // SUBLANES, SUBLANES, hd), jnp.uint32),
                            pltpu.SemaphoreType.DMA((2,))],
        ),
        compiler_params=_cparams(1),
        name="moe_combine_ln",
    )(slot, y_pk.reshape(p // SUBLANES, SUBLANES, hd), gates, h, g.reshape(1, d), b.reshape(1, d))


def _moe_ln(h, h_pk, layer, router_w, router_b, w_gate_up_all, b_gate_up, w_down_all, b_down, g, b):
    tm = MOE_TILE
    top_idx, gates = _router(h_pk, router_w, router_b)
    slot, row_src, tile_expert, n_used = _route_plan(top_idx, tm)
    wgu = _gate_up_weights(w_gate_up_all, layer)
    bgu = jnp.concatenate([b_gate_up[..., 0::2], b_gate_up[..., 1::2]], axis=-1).astype(F32)
    bgu = bgu.reshape(N_EXPERTS, 1, 2 * D_EXPERT)
    wd = _layer_bf16(w_down_all, layer, "cast_w_down").reshape(N_EXPERTS, D_EXPERT, D_MODEL)
    bd = b_down.astype(F32).reshape(N_EXPERTS, 1, D_MODEL)
    y_pk = _experts(h_pk, row_src, tile_expert, n_used, wgu, bgu, wd, bd, tm)
    return _combine_ln(slot, y_pk, gates, h, g, b)


def _even_mixer(h_bf, i, w_in_all, pool_w, pool_scale, conv_w, conv_b, dt_bias, a_log, d_skip, norm_w,
                w_out_all, batch, seq):
    w_in_bf = _layer_bf16(w_in_all, i, "cast_even_w_in")
    w_dt = jnp.pad(w_in_all[i, :, EVEN_MAIN:], ((0, 0), (0, LANES - SSM_HEADS))).astype(BF16)
    u_main = _matmul([h_bf], w_in_bf, EVEN_MAIN, 512, F32, "even_in_proj")
    dt_raw = _matmul([h_bf], w_dt, LANES, LANES, F32, "even_dt_proj")
    y_pool = _pool(u_main, pool_w.astype(BF16), pool_scale.astype(F32), seq)
    y_ssd = _ssd(u_main, dt_raw, conv_w, conv_b, dt_bias, a_log, d_skip, norm_w, batch, seq)
    w_out_bf = _layer_bf16(w_out_all, i, "cast_even_w_out")
    return _matmul([y_pool, y_ssd], w_out_bf, D_MODEL, 512, F32, "even_out_proj")


def _attn_mixer(h_bf, i, w_qkv_all, w_out_all, bias_tabs, batch, seq):
    w_qkv_bf = _layer_bf16(w_qkv_all, i, "cast_attn_w_qkv")
    qkv_heads = _matmul([h_bf], w_qkv_bf, 3 * D_MODEL, 512, F32, "attn_qkv_proj", head_out=True,
                        scaled_cols=D_MODEL, scale=ATTN_HEAD_DIM ** -0.5)
    o = _attention(qkv_heads, bias_tabs, batch, seq)
    w_out_bf = _layer_bf16(w_out_all, i, "cast_attn_w_out")
    return _matmul([o], w_out_bf, D_MODEL, 512, F32, "attn_out_proj")


def kernel(x, rel_bias, even_w_in, pool_w, pool_scale, conv_w, conv_b, dt_bias, a_log, d_skip, ssm_norm_w, even_w_out, attn_w_qkv, attn_w_out, ln1_g, ln1_b, ln2_g, ln2_b, router_w, router_b, w_gate_up, b_gate_up, w_down, b_down):
    batch, seq, d = x.shape
    t = batch * seq
    h = x.reshape(t, d).astype(F32)
    h_bf = _cast_rows_bf16(h, 0, t, "cast_x")
    bias_tabs = _attn_bias_tables(rel_bias)
    for layer in range(DEPTH):
        i = layer // 2
        if layer % 2 == 0:
            mix = _even_mixer(h_bf, i, even_w_in, pool_w[i], pool_scale[i], conv_w[i], conv_b[i],
                              dt_bias[i], a_log[i], d_skip[i], ssm_norm_w[i], even_w_out, batch, seq)
        else:
            mix = _attn_mixer(h_bf, i, attn_w_qkv, attn_w_out, bias_tabs, batch, seq)
        h, h_pk = _residual_ln(h, mix, ln1_g[layer], ln1_b[layer])
        h, h_bf = _moe_ln(h, h_pk, layer, router_w[layer], router_b[layer], w_gate_up, b_gate_up[layer],
                          w_down, b_down[layer], ln2_g[layer], ln2_b[layer])
    return h.reshape(batch, seq, d).astype(x.dtype)
```

```python
import functools
import math

import numpy as np
import jax
import jax.numpy as jnp
from jax import lax
from jax.experimental import pallas as pl
from jax.experimental.pallas import tpu as pltpu

F32 = jnp.float32
BF16 = jnp.bfloat16

D_MODEL = 4096
DEPTH = 4
POOL_WINDOWS = (2, 4, 8, 16)
POOL_GROUP = D_MODEL // 8
POOL_WIDTH = POOL_GROUP * len(POOL_WINDOWS)
SSM_HEAD_DIM = 64
SSM_INNER = D_MODEL // 2
SSM_HEADS = SSM_INNER // SSM_HEAD_DIM
SSM_GROUPS = 4
SSM_STATE = 128
SSM_CONV = 4
SSM_CHUNK = 128
SSM_CONV_DIM = SSM_INNER + 2 * SSM_GROUPS * SSM_STATE
EVEN_MAIN = POOL_WIDTH + SSM_INNER + SSM_CONV_DIM
ATTN_HEAD_DIM = 128
ATTN_HEADS = D_MODEL // ATTN_HEAD_DIM
DILATED = ((128, 1), (512, 4), (2048, 16))
ATTN_BLOCK = 128
ATTN_SUPER = ATTN_BLOCK * 16
REL_BUCKETS = 32
REL_MAX_DIST = 2048
N_EXPERTS = 32
TOP_K = 4
D_EXPERT = 256
SWIGLU_LIMIT = 7.0
SWIGLU_ALPHA = 1.702
DEEPNORM_ALPHA = (2 * DEPTH) ** 0.25
LN_EPS = 1e-5
RMS_EPS = 1e-5

LANES = 128
NEG = -1e30
VMEM_LIMIT = 56 * 1024 * 1024
MOE_TILE = 256


def _cparams(n_axes):
    return pltpu.CompilerParams(dimension_semantics=("arbitrary",) * n_axes,
                                vmem_limit_bytes=VMEM_LIMIT)


def _mm_body(*refs, n_lhs, k_bounds, head_out, scaled_blocks, scale):
    lhs = refs[:n_lhs]
    w_ref = refs[n_lhs]
    o_ref = refs[n_lhs + 1]
    acc = None
    for l_ref, (k0, k1) in zip(lhs, k_bounds):
        part = jnp.dot(l_ref[...], w_ref[k0:k1, :], preferred_element_type=F32)
        acc = part if acc is None else acc + part
    if scaled_blocks:
        acc = acc * jnp.where(pl.program_id(1) < scaled_blocks, scale, 1.0)
    if head_out:
        for j in range(o_ref.shape[0]):
            o_ref[j] = acc[:, j * LANES:(j + 1) * LANES].astype(o_ref.dtype)
    else:
        o_ref[...] = acc.astype(o_ref.dtype)


def _matmul(lhs_list, w, n_out, tn, out_dtype, name, head_out=False, scaled_cols=0, scale=1.0):
    assert scaled_cols % tn == 0
    m = lhs_list[0].shape[0]
    k = w.shape[0]
    tm = min(1024, m)
    k_bounds = []
    k0 = 0
    for l in lhs_list:
        k_bounds.append((k0, k0 + l.shape[1]))
        k0 += l.shape[1]
    assert k0 == k and m % tm == 0 and n_out % tn == 0
    in_specs = [pl.BlockSpec((tm, l.shape[1]), lambda i, j: (i, 0)) for l in lhs_list]
    in_specs.append(pl.BlockSpec((k, tn), lambda i, j: (0, j)))
    if head_out:
        out_shape = jax.ShapeDtypeStruct((n_out // LANES, m, LANES), out_dtype)
        out_spec = pl.BlockSpec((tn // LANES, tm, LANES), lambda i, j: (j, i, 0))
    else:
        out_shape = jax.ShapeDtypeStruct((m, n_out), out_dtype)
        out_spec = pl.BlockSpec((tm, tn), lambda i, j: (i, j))
    return pl.pallas_call(
        functools.partial(_mm_body, n_lhs=len(lhs_list), k_bounds=tuple(k_bounds), head_out=head_out,
                          scaled_blocks=scaled_cols // tn, scale=scale),
        out_shape=out_shape,
        grid=(m // tm, n_out // tn),
        in_specs=in_specs,
        out_specs=out_spec,
        compiler_params=_cparams(2),
        name=name,
    )(*lhs_list, w)


def _cast_body(w_ref, o_ref):
    o_ref[...] = w_ref[...].astype(o_ref.dtype)


def _cast_rows_bf16(w2d, row0, nrows, name):
    n = w2d.shape[1]
    tk = 256
    assert row0 % tk == 0 and nrows % tk == 0
    return pl.pallas_call(
        _cast_body,
        out_shape=jax.ShapeDtypeStruct((nrows, n), BF16),
        grid=(nrows // tk,),
        in_specs=[pl.BlockSpec((tk, n), lambda i: (row0 // tk + i, 0))],
        out_specs=pl.BlockSpec((tk, n), lambda i: (i, 0)),
        compiler_params=_cparams(1),
        name=name,
    )(w2d)


def _layer_bf16(w_stacked, layer, name):
    n = w_stacked.shape[-1]
    rows = math.prod(w_stacked.shape[1:-1])
    return _cast_rows_bf16(w_stacked.reshape(-1, n), layer * rows, rows, name)


HALF = D_MODEL // 2


def _pack_pairs(y):
    lo = lax.bitcast_convert_type(y[:, :HALF].astype(BF16).astype(F32), jnp.uint32)
    hi = lax.bitcast_convert_type(y[:, HALF:].astype(BF16).astype(F32), jnp.uint32)
    return hi | (lo >> 16)


def _unpack_pairs(w):
    lo = lax.bitcast_convert_type(w << 16, F32)
    hi = lax.bitcast_convert_type(w & jnp.uint32(0xFFFF0000), F32)
    return lo, hi


def _layer_norm_rows(x, g, b):
    mu = jnp.mean(x, axis=-1, keepdims=True)
    xc = x - mu
    var = jnp.mean(xc * xc, axis=-1, keepdims=True)
    return xc * lax.rsqrt(var + LN_EPS) * g + b


def _ln_body(h_ref, mix_ref, g_ref, b_ref, o_ref, opk_ref):
    y = _layer_norm_rows(DEEPNORM_ALPHA * h_ref[...] + mix_ref[...].astype(F32), g_ref[...], b_ref[...])
    o_ref[...] = y
    opk_ref[...] = _pack_pairs(y)


def _residual_ln(h, mix, g, b):
    t, d = h.shape
    tm = min(256, t)
    row = pl.BlockSpec((tm, d), lambda i: (i, 0))
    vec = pl.BlockSpec((1, d), lambda i: (0, 0))
    return pl.pallas_call(
        _ln_body,
        out_shape=(jax.ShapeDtypeStruct((t, d), F32), jax.ShapeDtypeStruct((t, HALF), jnp.uint32)),
        grid=(t // tm,),
        in_specs=[row, row, vec, vec],
        out_specs=(row, pl.BlockSpec((tm, HALF), lambda i: (i, 0))),
        compiler_params=_cparams(1),
        name="residual_ln",
    )(h, mix, g.reshape(1, d), b.reshape(1, d))


POOL_HALO = 16


def _pool_body(u_ref, halo_ref, w_ref, scale_ref, o_ref, buf, *, ts, tiles_per_seq):
    i = pl.program_id(0)
    first = (i % tiles_per_seq) == 0
    buf[0:POOL_HALO, :] = jnp.where(first, 0.0, halo_ref[...])
    buf[POOL_HALO:POOL_HALO + ts, :] = u_ref[...]
    pos = (i % tiles_per_seq) * ts + lax.broadcasted_iota(jnp.int32, (ts, 1), 0)
    for g, w in enumerate(POOL_WINDOWS):
        c0, c1 = g * POOL_GROUP, (g + 1) * POOL_GROUP
        cur = buf[POOL_HALO:POOL_HALO + ts, c0:c1]
        acc = cur
        for j in range(1, w):
            acc = acc + buf[POOL_HALO - j:POOL_HALO - j + ts, c0:c1]
        count = jnp.minimum(pos + 1, w).astype(F32)
        d = acc / count - cur
        y = jnp.dot(d.astype(BF16), w_ref[g], preferred_element_type=F32)
        o_ref[:, c0:c1] = (y * scale_ref[:, c0:c1]).astype(o_ref.dtype)


def _pool(u_main, pool_w_bf, pool_scale, seq):
    t = u_main.shape[0]
    ts = min(512, seq)
    tiles_per_seq = seq // ts
    hb = ts // POOL_HALO
    return pl.pallas_call(
        functools.partial(_pool_body, ts=ts, tiles_per_seq=tiles_per_seq),
        out_shape=jax.ShapeDtypeStruct((t, POOL_WIDTH), BF16),
        grid=(t // ts,),
        in_specs=[
            pl.BlockSpec((ts, POOL_WIDTH), lambda i: (i, 0)),
            pl.BlockSpec((POOL_HALO, POOL_WIDTH), lambda i: (jnp.maximum(i * hb - 1, 0), 0)),
            pl.BlockSpec((len(POOL_WINDOWS), POOL_GROUP, POOL_GROUP), lambda i: (0, 0, 0)),
            pl.BlockSpec((1, POOL_WIDTH), lambda i: (0, 0)),
        ],
        out_specs=pl.BlockSpec((ts, POOL_WIDTH), lambda i: (i, 0)),
        scratch_shapes=[pltpu.VMEM((ts + POOL_HALO, POOL_WIDTH), F32)],
        compiler_params=_cparams(1),
        name="pool_mixer",
    )(u_main, u_main, pool_w_bf, pool_scale.reshape(1, POOL_WIDTH))


CONV_HALO = 8
PAIR = 2 * SSM_HEAD_DIM
N_PAIRS = SSM_HEADS // 2
PAIRS_PER_GROUP = N_PAIRS // SSM_GROUPS


def _softplus(x):
    return jnp.maximum(x, 0.0) + jnp.log1p(jnp.exp(-jnp.abs(x)))


def _silu(x):
    return x * jax.nn.sigmoid(x)


def _ssd_body(z_ref, xs_ref, bc_ref, dt_ref, convw_ref, convb_ref, dtb_row_ref, dtb_col_ref,
              alog_row_ref, alog_col_ref, dskip_ref, normw_ref, o_ref,
              buf, xbc, carry, state, ybuf):
    L = SSM_CHUNK
    c = pl.program_id(1)

    @pl.when(c == 0)
    def _():
        carry[...] = jnp.zeros_like(carry)
        state[...] = jnp.zeros_like(state)

    buf[0:CONV_HALO, :] = carry[...]
    buf[CONV_HALO:CONV_HALO + L, 0:SSM_INNER] = xs_ref[...]
    buf[CONV_HALO:CONV_HALO + L, SSM_INNER:SSM_CONV_DIM] = bc_ref[...]
    carry[...] = buf[L:L + CONV_HALO, :]
    cw = 512
    for j in range(SSM_CONV_DIM // cw):
        cs = slice(j * cw, (j + 1) * cw)
        acc = convb_ref[:, cs] + buf[CONV_HALO - 3:CONV_HALO - 3 + L, cs] * convw_ref[0:1, cs]
        for i in range(1, SSM_CONV):
            off = CONV_HALO - 3 + i
            acc = acc + buf[off:off + L, cs] * convw_ref[i:i + 1, cs]
        xbc[:, cs] = _silu(acc)

    ri = lax.broadcasted_iota(jnp.int32, (L, L), 0)
    ci = lax.broadcasted_iota(jnp.int32, (L, L), 1)
    tril = ri >= ci
    dt_blk = dt_ref[...]
    dt = _softplus(dt_blk + dtb_row_ref[...])
    da = dt * (-jnp.exp(alog_row_ref[...]))
    acum = jnp.dot(tril.astype(F32), da, precision=lax.Precision.HIGHEST,
                   preferred_element_type=F32)
    dt_t = _softplus(dt_blk.T + dtb_col_ref[...])
    da_t = dt_t * (-jnp.exp(alog_col_ref[...]))
    acum_t = jnp.dot(da_t, (ri <= ci).astype(F32), precision=lax.Precision.HIGHEST,
                     preferred_element_type=F32)
    w_t = jnp.exp(acum_t[:, L - 1:L] - acum_t) * dt_t
    eac = jnp.exp(acum)
    cd_row = jnp.exp(acum[L - 1:L, :])

    lane = lax.broadcasted_iota(jnp.int32, (L, PAIR), 1)
    lane_row = lax.broadcasted_iota(jnp.int32, (1, PAIR), 1)
    nt = (((1,), (1,)), ((), ()))
    for g in range(SSM_GROUPS):
        b0 = SSM_INNER + g * SSM_STATE
        c0 = SSM_INNER + SSM_GROUPS * SSM_STATE + g * SSM_STATE
        bm = xbc[:, b0:b0 + SSM_STATE]
        cm = xbc[:, c0:c0 + SSM_STATE]
        cb = lax.dot_general(cm.astype(BF16), bm.astype(BF16), nt, preferred_element_type=F32)
        bm_t = bm.T
        for j in range(PAIRS_PER_GROUP):
            p = g * PAIRS_PER_GROUP + j
            xs_pair = xbc[:, p * PAIR:(p + 1) * PAIR].astype(BF16)
            st = state[p]
            lhs_y, lhs_s = [], []
            for h in (2 * p, 2 * p + 1):
                seg = acum[:, h:h + 1] - acum_t[h:h + 1, :]
                dec = jnp.exp(jnp.where(tril, seg, -jnp.inf))
                m_h = cb * dec * dt_t[h:h + 1, :]
                e_h = eac[:, h:h + 1] * cm
                lhs_y.append(jnp.concatenate([m_h, e_h], axis=1).astype(BF16))
                lhs_s.append((bm_t * w_t[h:h + 1, :]).astype(BF16))
            rhs = jnp.concatenate([xs_pair, st.astype(BF16)], axis=0)
            r = jnp.dot(jnp.concatenate(lhs_y, axis=0), rhs, preferred_element_type=F32)
            ybuf[:, p * PAIR:(p + 1) * PAIR] = jnp.where(lane < SSM_HEAD_DIM, r[0:L], r[L:2 * L])
            s_new = jnp.dot(jnp.concatenate(lhs_s, axis=0), xs_pair, preferred_element_type=F32)
            contrib = jnp.where(lane < SSM_HEAD_DIM, s_new[0:L], s_new[L:2 * L])
            cdp = jnp.where(lane_row < SSM_HEAD_DIM, cd_row[:, 2 * p:2 * p + 1], cd_row[:, 2 * p + 1:2 * p + 2])
            state[p] = st * cdp + contrib

    gw = SSM_INNER // SSM_GROUPS
    for g in range(SSM_GROUPS):
        cs = slice(g * gw, (g + 1) * gw)
        y = ybuf[:, cs] + xbc[:, cs] * dskip_ref[:, cs]
        y = y * _silu(z_ref[:, cs])
        ms = jnp.mean(y * y, axis=-1, keepdims=True)
        o_ref[:, cs] = (y * lax.rsqrt(ms + RMS_EPS) * normw_ref[:, cs]).astype(o_ref.dtype)


def _ssd(u_main, dt_raw, conv_w, conv_b, dt_bias, a_log, d_skip, norm_w, batch, seq):
    t = u_main.shape[0]
    L = SSM_CHUNK
    nc = seq // L
    pad = LANES - SSM_HEADS
    dtb = jnp.pad(dt_bias.astype(F32), (0, pad))
    alog = jnp.pad(a_log.astype(F32), (0, pad))
    dskip = jnp.repeat(d_skip.astype(F32), SSM_HEAD_DIM).reshape(1, SSM_INNER)
    zcol = POOL_WIDTH // SSM_INNER
    xcol = (POOL_WIDTH + SSM_INNER) // SSM_INNER
    bc_w = 2 * SSM_GROUPS * SSM_STATE
    bcol = (POOL_WIDTH + 2 * SSM_INNER) // bc_w
    rowmap = lambda col: (lambda b, c: (b * nc + c, col))
    full = lambda shape: pl.BlockSpec(shape, lambda b, c: (0,) * len(shape))
    return pl.pallas_call(
        _ssd_body,
        out_shape=jax.ShapeDtypeStruct((t, SSM_INNER), BF16),
        grid=(batch, nc),
        in_specs=[
            pl.BlockSpec((L, SSM_INNER), rowmap(zcol)),
            pl.BlockSpec((L, SSM_INNER), rowmap(xcol)),
            pl.BlockSpec((L, bc_w), rowmap(bcol)),
            pl.BlockSpec((L, LANES), rowmap(0)),
            full((SSM_CONV, SSM_CONV_DIM)),
            full((1, SSM_CONV_DIM)),
            full((1, LANES)),
            full((LANES, 1)),
            full((1, LANES)),
            full((LANES, 1)),
            full((1, SSM_INNER)),
            full((1, SSM_INNER)),
        ],
        out_specs=pl.BlockSpec((L, SSM_INNER), rowmap(0)),
        scratch_shapes=[
            pltpu.VMEM((L + CONV_HALO, SSM_CONV_DIM), F32),
            pltpu.VMEM((L, SSM_CONV_DIM), F32),
            pltpu.VMEM((CONV_HALO, SSM_CONV_DIM), F32),
            pltpu.VMEM((N_PAIRS, SSM_STATE, PAIR), F32),
            pltpu.VMEM((L, SSM_INNER), F32),
        ],
        compiler_params=_cparams(2),
        name="ssd_mixer",
    )(u_main, u_main, u_main, dt_raw, conv_w.astype(F32), conv_b.reshape(1, SSM_CONV_DIM).astype(F32),
      dtb.reshape(1, LANES), dtb.reshape(LANES, 1), alog.reshape(1, LANES), alog.reshape(LANES, 1),
      dskip, norm_w.reshape(1, SSM_INNER).astype(F32))


def _t5_bucket(dist):
    max_exact = REL_BUCKETS // 2
    d = np.maximum(dist, 1).astype(np.float32)
    large = max_exact + (np.log(d / max_exact) / np.log(REL_MAX_DIST / max_exact)
                         * (REL_BUCKETS - max_exact)).astype(np.int32)
    large = np.minimum(large, REL_BUCKETS - 1)
    return np.where(dist < max_exact, dist, large).astype(np.int32)


def _attn_bias_tables(rel_bias):
    blk = ATTN_BLOCK
    delta = blk + np.arange(blk)[:, None] - np.arange(2 * blk)[None, :]
    tabs = []
    for window, dil in DILATED:
        n_back = window // dil
        assert n_back <= blk
        band = (delta >= 0) & (delta <= n_back)
        bucket = _t5_bucket(np.clip(delta, 0, n_back) * dil)
        rel = jnp.zeros((ATTN_HEADS,) + bucket.shape, F32)
        for bkt in np.unique(bucket[band]):
            rel = jnp.where((bucket == bkt)[None], rel_bias[bkt].astype(F32)[:, None, None], rel)
        tabs.append(jnp.where(band[None], rel, NEG))
    return jnp.stack(tabs, 0)


ATTN_GROUP = 16


def _attn_body(q_ref, kp_ref, kc_ref, vp_ref, vc_ref, bias_ref, o_ref, m_ref, l_ref, acc_ref):
    blk = ATTN_BLOCK
    grp = ATTN_GROUP
    seq_first = pl.program_id(2) == 0
    nt = (((1,), (1,)), ((), ()))
    col = lax.broadcasted_iota(jnp.int32, (blk, 2 * blk), 1)

    def rows(start, n, d):
        if d > 1:
            return pl.ds(start, n, stride=d)
        return pl.ds(start if isinstance(start, int) else pl.multiple_of(start, blk), n)

    for bi, (_, d) in enumerate(DILATED):
        span = blk * d
        nblk = ATTN_SUPER // span
        bias = bias_ref[bi]
        bias_first = jnp.where(jnp.logical_and(seq_first, col < blk), NEG, bias)

        def block(qs, first, bi=bi, d=d, span=span, bias=bias, bias_first=bias_first):
            qr = rows(qs, blk, d)
            q = q_ref[qr, :].astype(BF16)
            if first:
                pr = rows(ATTN_SUPER - span + qs, blk, d)
                k = jnp.concatenate([kp_ref[pr, :], kc_ref[qr, :]], axis=0)
                v = jnp.concatenate([vp_ref[pr, :], vc_ref[qr, :]], axis=0)
                b = bias_first
            else:
                wr = rows(qs - span, 2 * blk, d)
                k = kc_ref[wr, :]
                v = vc_ref[wr, :]
                b = bias
            s = lax.dot_general(q, k.astype(BF16), nt, preferred_element_type=F32) + b
            rm = jnp.max(s, axis=1, keepdims=True)
            p = jnp.exp(s - rm)
            rs = jnp.sum(p, axis=1, keepdims=True)
            pv = jnp.dot(p.astype(BF16), v.astype(BF16), preferred_element_type=F32)
            m_ref[bi, qr, :] = jnp.broadcast_to(rm, (blk, LANES))
            l_ref[bi, qr, :] = jnp.broadcast_to(rs, (blk, LANES))
            acc_ref[bi, qr, :] = pv

        if nblk == 1:
            def body(i, c, block=block):
                for u in range(grp):
                    block(i * grp + u, True)
                return c
            lax.fori_loop(0, d // grp, body, 0)
        elif d == 1:
            block(0, True)
            for u in range(1, grp):
                block(u * span, False)

            def body(i, c, block=block, span=span):
                for u in range(grp):
                    block((i * grp + u) * span, False)
                return c
            lax.fori_loop(1, nblk // grp, body, 0)
        else:
            rpi = max(grp // nblk, 1)

            def body(i, c, block=block, span=span, nblk=nblk, rpi=rpi):
                for u in range(rpi):
                    r = i * rpi + u
                    block(r, True)
                    for n in range(1, nblk):
                        block(n * span + r, False)
                return c
            lax.fori_loop(0, d // rpi, body, 0)

    def merge(c, carry):
        rr = pl.ds(pl.multiple_of(c * blk, blk), blk)
        ms = [m_ref[i, rr, :] for i in range(len(DILATED))]
        mm = functools.reduce(jnp.maximum, ms)
        ws = [jnp.exp(m - mm) for m in ms]
        den = functools.reduce(lambda a, b: a + b, [w * l_ref[i, rr, :] for i, w in enumerate(ws)])
        num = functools.reduce(lambda a, b: a + b, [w * acc_ref[i, rr, :] for i, w in enumerate(ws)])
        o_ref[rr, :] = (num / den).astype(o_ref.dtype)
        return carry

    lax.fori_loop(0, ATTN_SUPER // blk, merge, 0)


def _attention(qkv_heads, bias_tabs, batch, seq):
    t = qkv_heads.shape[1]
    hn = ATTN_HEADS
    sup = ATTN_SUPER
    nst = seq // sup
    assert seq % sup == 0
    cur = lambda off: (lambda b, h, s: (off + h, b * nst + s, 0))
    prev = lambda off: (lambda b, h, s: (off + h, b * nst + jnp.maximum(s - 1, 0), 0))
    slab = lambda imap: pl.BlockSpec((None, sup, LANES), imap)
    return pl.pallas_call(
        _attn_body,
        out_shape=jax.ShapeDtypeStruct((t, D_MODEL), BF16),
        grid=(batch, hn, nst),
        in_specs=[
            slab(cur(0)),
            slab(prev(hn)), slab(cur(hn)),
            slab(prev(2 * hn)), slab(cur(2 * hn)),
            pl.BlockSpec((len(DILATED), None, ATTN_BLOCK, 2 * ATTN_BLOCK), lambda b, h, s: (0, h, 0, 0)),
        ],
        out_specs=pl.BlockSpec((sup, LANES), lambda b, h, s: (b * nst + s, h)),
        scratch_shapes=[pltpu.VMEM((len(DILATED), sup, LANES), F32)] * 3,
        compiler_params=_cparams(3),
        name="dilated_attention",
    )(qkv_heads, qkv_heads, qkv_heads, qkv_heads, qkv_heads, bias_tabs)


def _router_body(h_ref, w_ref, b_ref, idx_ref, gate_ref):
    lo, hi = _unpack_pairs(h_ref[...])
    logits = (jnp.dot(lo.astype(BF16), w_ref[:HALF, :], preferred_element_type=F32)
              + jnp.dot(hi.astype(BF16), w_ref[HALF:, :], preferred_element_type=F32) + b_ref[...])
    lane = lax.broadcasted_iota(jnp.int32, logits.shape, 1)
    vals, idxs = [], []
    for _ in range(TOP_K):
        m = jnp.max(logits, axis=1, keepdims=True)
        idx = jnp.min(jnp.where(logits == m, lane, LANES), axis=1, keepdims=True)
        vals.append(m)
        idxs.append(idx)
        logits = jnp.where(lane == idx, -jnp.inf, logits)
    es = [jnp.exp(v - vals[0]) for v in vals]
    den = es[0]
    for e in es[1:]:
        den = den + e
    idx_out = jnp.zeros(logits.shape, jnp.int32)
    gate_out = jnp.zeros(logits.shape, F32)
    for k in range(TOP_K):
        idx_out = jnp.where(lane == k, idxs[k], idx_out)
        gate_out = jnp.where(lane == k, es[k] / den, gate_out)
    idx_ref[...] = idx_out
    gate_ref[...] = gate_out


def _router(h_pk, router_w, router_b):
    t, hd = h_pk.shape
    d = 2 * hd
    tm = min(1024, t)
    pad = LANES - N_EXPERTS
    w = jnp.pad(router_w, ((0, 0), (0, pad))).astype(BF16)
    b = jnp.pad(router_b.astype(F32), (0, pad), constant_values=NEG).reshape(1, LANES)
    idx, gates = pl.pallas_call(
        _router_body,
        out_shape=(jax.ShapeDtypeStruct((t, LANES), jnp.int32), jax.ShapeDtypeStruct((t, LANES), F32)),
        grid=(t // tm,),
        in_specs=[pl.BlockSpec((tm, hd), lambda i: (i, 0)),
                  pl.BlockSpec((d, LANES), lambda i: (0, 0)),
                  pl.BlockSpec((1, LANES), lambda i: (0, 0))],
        out_specs=(pl.BlockSpec((tm, LANES), lambda i: (i, 0)), pl.BlockSpec((tm, LANES), lambda i: (i, 0))),
        compiler_params=_cparams(1),
        name="moe_router",
    )(h_pk, w, b)
    return idx, gates


def _rank_body(idx_ref, rank_ref, counts_ref, carry):
    @pl.when(pl.program_id(0) == 0)
    def _():
        carry[...] = jnp.zeros_like(carry)

    idx = idx_ref[...]
    tt = idx.shape[0]
    lane = lax.broadcasted_iota(jnp.int32, idx.shape, 1)
    hits = [idx[:, k:k + 1] == lane for k in range(TOP_K)]
    per_expert = functools.reduce(lambda a, b: a + b, [h.astype(F32) for h in hits])
    earlier = (lax.broadcasted_iota(jnp.int32, (tt, tt), 0) > lax.broadcasted_iota(jnp.int32, (tt, tt), 1))
    before = jnp.dot(earlier.astype(BF16), per_expert.astype(BF16), preferred_element_type=F32) + carry[...]
    out = jnp.zeros(idx.shape, jnp.int32)
    for k in range(TOP_K):
        rk = jnp.sum(jnp.where(hits[k], before, 0.0), axis=1, keepdims=True).astype(jnp.int32)
        out = jnp.where(lane == k, rk, out)
    rank_ref[...] = out
    carry[...] = carry[...] + jnp.sum(per_expert, axis=0, keepdims=True)
    counts_ref[...] = carry[...].astype(jnp.int32)


def _slot_body(idx_ref, rank_ref, starts_ref, slot_ref):
    idx = idx_ref[...]
    lane = lax.broadcasted_iota(jnp.int32, idx.shape, 1)
    starts = starts_ref[...].astype(F32)
    out = jnp.zeros(idx.shape, jnp.int32)
    for k in range(TOP_K):
        st = jnp.sum(jnp.where(idx[:, k:k + 1] == lane, starts, 0.0), axis=1, keepdims=True)
        out = jnp.where(lane == k, st.astype(jnp.int32), out)
    slot_ref[...] = out + rank_ref[...]


def _route_plan(top_idx, tm):
    t = top_idx.shape[0]
    a = t * TOP_K
    tt = min(512, t)
    blk = pl.BlockSpec((tt, LANES), lambda i: (i, 0))
    vec = pl.BlockSpec((1, LANES), lambda i: (0, 0))
    rank, counts = pl.pallas_call(
        _rank_body,
        out_shape=(jax.ShapeDtypeStruct((t, LANES), jnp.int32), jax.ShapeDtypeStruct((1, LANES), jnp.int32)),
        grid=(t // tt,),
        in_specs=[blk],
        out_specs=(blk, vec),
        scratch_shapes=[pltpu.VMEM((1, LANES), F32)],
        compiler_params=_cparams(1),
        name="moe_rank",
    )(top_idx)
    counts = counts[0, :N_EXPERTS]
    padded = ((counts + tm - 1) // tm) * tm
    ends = jnp.cumsum(padded)
    starts = ends - padded
    slot = pl.pallas_call(
        _slot_body,
        out_shape=jax.ShapeDtypeStruct((t, LANES), jnp.int32),
        grid=(t // tt,),
        in_specs=[blk, blk, vec],
        out_specs=blk,
        compiler_params=_cparams(1),
        name="moe_slot",
    )(top_idx, rank, jnp.pad(starts, (0, LANES - N_EXPERTS)).reshape(1, LANES).astype(jnp.int32))
    slot = slot[:, :TOP_K].reshape(a)
    p = a + N_EXPERTS * tm
    token = jnp.arange(a, dtype=jnp.int32) // TOP_K
    row_src = jnp.zeros((p,), jnp.int32).at[slot].set(token, unique_indices=True)
    tile_start = jnp.arange(p // tm, dtype=jnp.int32) * tm
    tile_expert = jnp.sum((ends[None, :] <= tile_start[:, None]).astype(jnp.int32), axis=1)
    tile_expert = jnp.minimum(tile_expert, N_EXPERTS - 1).astype(jnp.int32)
    n_used = (ends[-1] // tm).astype(jnp.int32).reshape(1)
    return slot, row_src, tile_expert, n_used


def _deinterleave_body(w_ref, perm_ref, o_ref):
    o_ref[...] = jnp.dot(w_ref[...].astype(BF16), perm_ref[...], preferred_element_type=F32).astype(o_ref.dtype)


def _gate_up_weights(w_gate_up_all, layer):
    _, ne, d, n = w_gate_up_all.shape
    tk = 1024
    src = np.concatenate([np.arange(0, n, 2), np.arange(1, n, 2)])
    perm = np.zeros((n, n), np.float32)
    perm[src, np.arange(n)] = 1.0
    return pl.pallas_call(
        _deinterleave_body,
        out_shape=jax.ShapeDtypeStruct((ne, d, n), BF16),
        grid=(ne, d // tk),
        in_specs=[pl.BlockSpec((None, None, tk, n), lambda e, k: (layer, e, k, 0)),
                  pl.BlockSpec((n, n), lambda e, k: (0, 0))],
        out_specs=pl.BlockSpec((None, tk, n), lambda e, k: (e, k, 0)),
        compiler_params=_cparams(2),
        name="moe_gate_up_prep",
    )(w_gate_up_all, jnp.asarray(perm, BF16))


SUBLANES = 8


def _start_row_gathers(idx_of, n_rows, src3_ref, dst_of, sem):
    def group(gi, c):
        for u in range(SUBLANES):
            row = idx_of(gi * SUBLANES + u)
            pltpu.make_async_copy(src3_ref.at[row >> 3, pl.ds(row & 7, 1)], dst_of(gi, u), sem).start()
        return c

    lax.fori_loop(0, n_rows // SUBLANES, group, 0)


def _expert_body(te_ref, nused_ref, src_ref, h_ref, wgu_ref, bgu_ref, wd_ref, bd_ref, o_ref,
                 xbuf, sem, *, tm):
    i = pl.program_id(0)
    n_used = nused_ref[0]

    def gather(tile, buf):
        base = tile * tm
        _start_row_gathers(lambda r: src_ref[base + r], tm, h_ref,
                           lambda gi, u: xbuf.at[buf, gi, pl.ds(u, 1)], sem.at[buf])

    def gather_inline(tile, buf, r_lo, r_hi):
        base = tile * tm
        for r in range(r_lo, r_hi):
            row = src_ref[base + r]
            pltpu.make_async_copy(h_ref.at[row >> 3, pl.ds(row & 7, 1)],
                                  xbuf.at[buf, r // SUBLANES, pl.ds(r % SUBLANES, 1)], sem.at[buf]).start()

    def wait_tile(buf):
        pltpu.make_async_copy(h_ref.at[pl.ds(0, tm // SUBLANES)], xbuf.at[buf], sem.at[buf]).wait()

    @pl.when(i == 0)
    def _():
        gather(0, 0)

    @pl.when(i < n_used)
    def _():
        buf = i % 2
        wait_tile(buf)
        lo, hi = _unpack_pairs(xbuf[buf].reshape(tm, HALF))
        hdn = (jnp.dot(lo.astype(BF16), wgu_ref[:HALF, :], preferred_element_type=F32)
               + jnp.dot(hi.astype(BF16), wgu_ref[HALF:, :], preferred_element_type=F32) + bgu_ref[...])
        gate = jnp.minimum(hdn[:, :D_EXPERT], SWIGLU_LIMIT)
        up = jnp.clip(hdn[:, D_EXPERT:], -SWIGLU_LIMIT, SWIGLU_LIMIT)
        act = (up + 1.0) * gate * jax.nn.sigmoid(SWIGLU_ALPHA * gate)
        gather_inline(i + 1, 1 - buf, 0, tm // 2)
        y = jnp.dot(act.astype(BF16), wd_ref[...], preferred_element_type=F32) + bd_ref[...]
        o_ref[...] = _pack_pairs(y)
        gather_inline(i + 1, 1 - buf, tm // 2, tm)

    @pl.when(i == n_used)
    def _():
        wait_tile(i % 2)

    @pl.when(i >= n_used)
    def _():
        o_ref[...] = jnp.zeros_like(o_ref)


def _experts(h_pk, row_src, tile_expert, n_used, wgu, bgu, wd, bd, tm):
    p = row_src.shape[0]
    t, hd = h_pk.shape
    d = 2 * hd
    return pl.pallas_call(
        functools.partial(_expert_body, tm=tm),
        out_shape=jax.ShapeDtypeStruct((p, hd), jnp.uint32),
        grid_spec=pltpu.PrefetchScalarGridSpec(
            num_scalar_prefetch=3,
            grid=(p // tm,),
            in_specs=[
                pl.BlockSpec(memory_space=pl.ANY),
                pl.BlockSpec((None, d, 2 * D_EXPERT), lambda i, te, nu, src: (te[i], 0, 0)),
                pl.BlockSpec((None, 1, 2 * D_EXPERT), lambda i, te, nu, src: (te[i], 0, 0)),
                pl.BlockSpec((None, D_EXPERT, d), lambda i, te, nu, src: (te[i], 0, 0)),
                pl.BlockSpec((None, 1, d), lambda i, te, nu, src: (te[i], 0, 0)),
            ],
            out_specs=pl.BlockSpec((tm, hd), lambda i, te, nu, src: (i, 0)),
            scratch_shapes=[pltpu.VMEM((2, tm // SUBLANES, SUBLANES, hd), jnp.uint32),
                            pltpu.SemaphoreType.DMA((2,))],
        ),
        compiler_params=_cparams(1),
        name="moe_experts",
    )(tile_expert, n_used, row_src, h_pk.reshape(t // SUBLANES, SUBLANES, hd), wgu, bgu, wd, bd)


COMBINE_GROUP = 32


def _combine_body(slot_ref, y_ref, gate_ref, h_ref, g_ref, b_ref, o_ref, obf_ref, ybuf, sem, *, tc):
    i = pl.program_id(0)
    n = pl.num_programs(0)

    def gather(tile, buf):
        base = tile * tc
        for k in range(TOP_K):
            _start_row_gathers(lambda r, k=k: slot_ref[(base + r) * TOP_K + k], tc, y_ref,
                               lambda gi, u, k=k: ybuf.at[buf, k, gi, pl.ds(u, 1)], sem.at[buf])

    @pl.when(i == 0)
    def _():
        gather(0, 0)

    buf = i % 2
    nbuf = 1 - buf
    for k in range(TOP_K):
        pltpu.make_async_copy(y_ref.at[pl.ds(0, tc // SUBLANES)], ybuf.at[buf, k], sem.at[buf]).wait()

    nxt_base = jnp.minimum(i + 1, n - 1) * tc
    grp = COMBINE_GROUP

    def group(gi, c):
        r0 = pl.multiple_of(gi * grp, grp)
        rows = pl.ds(r0, grp)
        srows = pl.ds(pl.multiple_of(gi * (grp // SUBLANES), grp // SUBLANES), grp // SUBLANES)
        gates = gate_ref[rows, :]
        ffn_lo = ffn_hi = None
        for k in range(TOP_K):
            lo, hi = _unpack_pairs(ybuf[buf, k, srows].reshape(grp, HALF))
            gk = gates[:, k:k + 1]
            ffn_lo = gk * lo if ffn_lo is None else ffn_lo + gk * lo
            ffn_hi = gk * hi if ffn_hi is None else ffn_hi + gk * hi
        ffn = jnp.concatenate([ffn_lo, ffn_hi], axis=1)
        y = _layer_norm_rows(DEEPNORM_ALPHA * h_ref[rows, :] + ffn, g_ref[...], b_ref[...])
        o_ref[rows, :] = y
        obf_ref[rows, :] = y.astype(BF16)
        for gg in range(grp // SUBLANES):
            for u in range(SUBLANES):
                for k in range(TOP_K):
                    s = slot_ref[(nxt_base + r0 + gg * SUBLANES + u) * TOP_K + k]
                    pltpu.make_async_copy(y_ref.at[s >> 3, pl.ds(s & 7, 1)],
                                          ybuf.at[nbuf, k, gi * (grp // SUBLANES) + gg, pl.ds(u, 1)],
                                          sem.at[nbuf]).start()
        return c

    lax.fori_loop(0, tc // grp, group, 0)

    @pl.when(i == n - 1)
    def _():
        for k in range(TOP_K):
            pltpu.make_async_copy(y_ref.at[pl.ds(0, tc // SUBLANES)], ybuf.at[nbuf, k], sem.at[nbuf]).wait()


def _combine_ln(slot, y_pk, gates, h, g, b):
    t, d = h.shape
    p, hd = y_pk.shape
    tc = min(128, t)
    row = lambda i, s: (i, 0)
    vec = lambda i, s: (0, 0)
    return pl.pallas_call(
        functools.partial(_combine_body, tc=tc),
        out_shape=(jax.ShapeDtypeStruct((t, d), F32), jax.ShapeDtypeStruct((t, d), BF16)),
        grid_spec=pltpu.PrefetchScalarGridSpec(
            num_scalar_prefetch=1,
            grid=(t // tc,),
            in_specs=[
                pl.BlockSpec(memory_space=pl.ANY),
                pl.BlockSpec((tc, LANES), row),
                pl.BlockSpec((tc, d), row),
                pl.BlockSpec((1, d), vec),
                pl.BlockSpec((1, d), vec),
            ],
            out_specs=(pl.BlockSpec((tc, d), row), pl.BlockSpec((tc, d), row)),
            scratch_shapes=[pltpu.VMEM((2, TOP_K, tc // SUBLANES, SUBLANES, hd), jnp.uint32),
                            pltpu.SemaphoreType.DMA((2,))],
        ),
        compiler_params=_cparams(1),
        name="moe_combine_ln",
    )(slot, y_pk.reshape(p // SUBLANES, SUBLANES, hd), gates, h, g.reshape(1, d), b.reshape(1, d))


def _moe_ln(h, h_pk, layer, router_w, router_b, w_gate_up_all, b_gate_up, w_down_all, b_down, g, b):
    tm = MOE_TILE
    top_idx, gates = _router(h_pk, router_w, router_b)
    slot, row_src, tile_expert, n_used = _route_plan(top_idx, tm)
    wgu = _gate_up_weights(w_gate_up_all, layer)
    bgu = jnp.concatenate([b_gate_up[..., 0::2], b_gate_up[..., 1::2]], axis=-1).astype(F32)
    bgu = bgu.reshape(N_EXPERTS, 1, 2 * D_EXPERT)
    wd = _layer_bf16(w_down_all, layer, "cast_w_down").reshape(N_EXPERTS, D_EXPERT, D_MODEL)
    bd = b_down.astype(F32).reshape(N_EXPERTS, 1, D_MODEL)
    y_pk = _experts(h_pk, row_src, tile_expert, n_used, wgu, bgu, wd, bd, tm)
    return _combine_ln(slot, y_pk, gates, h, g, b)


def _even_mixer(h_bf, i, w_in_all, pool_w, pool_scale, conv_w, conv_b, dt_bias, a_log, d_skip, norm_w,
                w_out_all, batch, seq):
    w_in_bf = _layer_bf16(w_in_all, i, "cast_even_w_in")
    w_dt = jnp.pad(w_in_all[i, :, EVEN_MAIN:], ((0, 0), (0, LANES - SSM_HEADS))).astype(BF16)
    u_main = _matmul([h_bf], w_in_bf, EVEN_MAIN, 512, F32, "even_in_proj")
    dt_raw = _matmul([h_bf], w_dt, LANES, LANES, F32, "even_dt_proj")
    y_pool = _pool(u_main, pool_w.astype(BF16), pool_scale.astype(F32), seq)
    y_ssd = _ssd(u_main, dt_raw, conv_w, conv_b, dt_bias, a_log, d_skip, norm_w, batch, seq)
    w_out_bf = _layer_bf16(w_out_all, i, "cast_even_w_out")
    return _matmul([y_pool, y_ssd], w_out_bf, D_MODEL, 512, BF16, "even_out_proj")


def _attn_mixer(h_bf, i, w_qkv_all, w_out_all, bias_tabs, batch, seq):
    w_qkv_bf = _layer_bf16(w_qkv_all, i, "cast_attn_w_qkv")
    qkv_heads = _matmul([h_bf], w_qkv_bf, 3 * D_MODEL, 512, F32, "attn_qkv_proj", head_out=True,
                        scaled_cols=D_MODEL, scale=ATTN_HEAD_DIM ** -0.5)
    o = _attention(qkv_heads, bias_tabs, batch, seq)
    w_out_bf = _layer_bf16(w_out_all, i, "cast_attn_w_out")
    return _matmul([o], w_out_bf, D_MODEL, 512, BF16, "attn_out_proj")


def kernel(x, rel_bias, even_w_in, pool_w, pool_scale, conv_w, conv_b, dt_bias, a_log, d_skip, ssm_norm_w, even_w_out, attn_w_qkv, attn_w_out, ln1_g, ln1_b, ln2_g, ln2_b, router_w, router_b, w_gate_up, b_gate_up, w_down, b_down):
    batch, seq, d = x.shape
    t = batch * seq
    h = x.reshape(t, d).astype(F32)
    h_bf = _cast_rows_bf16(h, 0, t, "cast_x")
    bias_tabs = _attn_bias_tables(rel_bias)
    for layer in range(DEPTH):
        i = layer // 2
        if layer % 2 == 0:
            mix = _even_mixer(h_bf, i, even_w_in, pool_w[i], pool_scale[i], conv_w[i], conv_b[i],
                              dt_bias[i], a_log[i], d_skip[i], ssm_norm_w[i], even_w_out, batch, seq)
        else:
            mix = _attn_mixer(h_bf, i, attn_w_qkv, attn_w_out, bias_tabs, batch, seq)
        h, h_pk = _residual_ln(h, mix, ln1_g[layer], ln1_b[layer])
        h, h_bf = _moe_ln(h, h_pk, layer, router_w[layer], router_b[layer], w_gate_up, b_gate_up[layer],
                          w_down, b_down[layer], ln2_g[layer], ln2_b[layer])
    return h.reshape(batch, seq, d).astype(x.dtype)
```

```python
import functools
import math

import numpy as np
import jax
import jax.numpy as jnp
from jax import lax
from jax.experimental import pallas as pl
from jax.experimental.pallas import tpu as pltpu

F32 = jnp.float32
BF16 = jnp.bfloat16

D_MODEL = 4096
DEPTH = 4
POOL_WINDOWS = (2, 4, 8, 16)
POOL_GROUP = D_MODEL // 8
POOL_WIDTH = POOL_GROUP * len(POOL_WINDOWS)
SSM_HEAD_DIM = 64
SSM_INNER = D_MODEL // 2
SSM_HEADS = SSM_INNER // SSM_HEAD_DIM
SSM_GROUPS = 4
SSM_STATE = 128
SSM_CONV = 4
SSM_CHUNK = 128
SSM_CONV_DIM = SSM_INNER + 2 * SSM_GROUPS * SSM_STATE
EVEN_MAIN = POOL_WIDTH + SSM_INNER + SSM_CONV_DIM
ATTN_HEAD_DIM = 128
ATTN_HEADS = D_MODEL // ATTN_HEAD_DIM
DILATED = ((128, 1), (512, 4), (2048, 16))
ATTN_BLOCK = 128
ATTN_SUPER = ATTN_BLOCK * 16
REL_BUCKETS = 32
REL_MAX_DIST = 2048
N_EXPERTS = 32
TOP_K = 4
D_EXPERT = 256
SWIGLU_LIMIT = 7.0
SWIGLU_ALPHA = 1.702
DEEPNORM_ALPHA = (2 * DEPTH) ** 0.25
LN_EPS = 1e-5
RMS_EPS = 1e-5

LANES = 128
NEG = -1e30
VMEM_LIMIT = 56 * 1024 * 1024
MOE_TILE = 256


def _cparams(n_axes):
    return pltpu.CompilerParams(dimension_semantics=("arbitrary",) * n_axes,
                                vmem_limit_bytes=VMEM_LIMIT)


def _mm_body(*refs, n_lhs, k_bounds, head_out, scaled_blocks, scale):
    lhs = refs[:n_lhs]
    w_ref = refs[n_lhs]
    o_ref = refs[n_lhs + 1]
    acc = None
    for l_ref, (k0, k1) in zip(lhs, k_bounds):
        part = jnp.dot(l_ref[...], w_ref[k0:k1, :], preferred_element_type=F32)
        acc = part if acc is None else acc + part
    if scaled_blocks:
        acc = acc * jnp.where(pl.program_id(1) < scaled_blocks, scale, 1.0)
    if head_out:
        for j in range(o_ref.shape[0]):
            o_ref[j] = acc[:, j * LANES:(j + 1) * LANES].astype(o_ref.dtype)
    else:
        o_ref[...] = acc.astype(o_ref.dtype)


def _matmul(lhs_list, w, n_out, tn, out_dtype, name, head_out=False, scaled_cols=0, scale=1.0):
    assert scaled_cols % tn == 0
    m = lhs_list[0].shape[0]
    k = w.shape[0]
    tm = min(1024, m)
    k_bounds = []
    k0 = 0
    for l in lhs_list:
        k_bounds.append((k0, k0 + l.shape[1]))
        k0 += l.shape[1]
    assert k0 == k and m % tm == 0 and n_out % tn == 0
    in_specs = [pl.BlockSpec((tm, l.shape[1]), lambda i, j: (i, 0)) for l in lhs_list]
    in_specs.append(pl.BlockSpec((k, tn), lambda i, j: (0, j)))
    if head_out:
        out_shape = jax.ShapeDtypeStruct((n_out // LANES, m, LANES), out_dtype)
        out_spec = pl.BlockSpec((tn // LANES, tm, LANES), lambda i, j: (j, i, 0))
    else:
        out_shape = jax.ShapeDtypeStruct((m, n_out), out_dtype)
        out_spec = pl.BlockSpec((tm, tn), lambda i, j: (i, j))
    return pl.pallas_call(
        functools.partial(_mm_body, n_lhs=len(lhs_list), k_bounds=tuple(k_bounds), head_out=head_out,
                          scaled_blocks=scaled_cols // tn, scale=scale),
        out_shape=out_shape,
        grid=(m // tm, n_out // tn),
        in_specs=in_specs,
        out_specs=out_spec,
        compiler_params=_cparams(2),
        name=name,
    )(*lhs_list, w)


def _cast_body(w_ref, o_ref):
    o_ref[...] = w_ref[...].astype(o_ref.dtype)


def _cast_rows_bf16(w2d, row0, nrows, name):
    n = w2d.shape[1]
    tk = 256
    assert row0 % tk == 0 and nrows % tk == 0
    return pl.pallas_call(
        _cast_body,
        out_shape=jax.ShapeDtypeStruct((nrows, n), BF16),
        grid=(nrows // tk,),
        in_specs=[pl.BlockSpec((tk, n), lambda i: (row0 // tk + i, 0))],
        out_specs=pl.BlockSpec((tk, n), lambda i: (i, 0)),
        compiler_params=_cparams(1),
        name=name,
    )(w2d)


def _layer_bf16(w_stacked, layer, name):
    n = w_stacked.shape[-1]
    rows = math.prod(w_stacked.shape[1:-1])
    return _cast_rows_bf16(w_stacked.reshape(-1, n), layer * rows, rows, name)


HALF = D_MODEL // 2


def _pack_pairs(y):
    lo = lax.bitcast_convert_type(y[:, :HALF].astype(BF16).astype(F32), jnp.uint32)
    hi = lax.bitcast_convert_type(y[:, HALF:].astype(BF16).astype(F32), jnp.uint32)
    return hi | (lo >> 16)


def _unpack_pairs(w):
    lo = lax.bitcast_convert_type(w << 16, F32)
    hi = lax.bitcast_convert_type(w & jnp.uint32(0xFFFF0000), F32)
    return lo, hi


def _layer_norm_rows(x, g, b):
    mu = jnp.mean(x, axis=-1, keepdims=True)
    xc = x - mu
    var = jnp.mean(xc * xc, axis=-1, keepdims=True)
    return xc * lax.rsqrt(var + LN_EPS) * g + b


def _ln_body(h_ref, mix_ref, g_ref, b_ref, o_ref, opk_ref):
    y = _layer_norm_rows(DEEPNORM_ALPHA * h_ref[...] + mix_ref[...].astype(F32), g_ref[...], b_ref[...])
    o_ref[...] = y
    opk_ref[...] = _pack_pairs(y)


def _residual_ln(h, mix, g, b):
    t, d = h.shape
    tm = min(256, t)
    row = pl.BlockSpec((tm, d), lambda i: (i, 0))
    vec = pl.BlockSpec((1, d), lambda i: (0, 0))
    return pl.pallas_call(
        _ln_body,
        out_shape=(jax.ShapeDtypeStruct((t, d), F32), jax.ShapeDtypeStruct((t, HALF), jnp.uint32)),
        grid=(t // tm,),
        in_specs=[row, row, vec, vec],
        out_specs=(row, pl.BlockSpec((tm, HALF), lambda i: (i, 0))),
        compiler_params=_cparams(1),
        name="residual_ln",
    )(h, mix, g.reshape(1, d), b.reshape(1, d))


POOL_HALO = 16


def _pool_body(u_ref, halo_ref, w_ref, scale_ref, o_ref, buf, *, ts, tiles_per_seq):
    i = pl.program_id(0)
    first = (i % tiles_per_seq) == 0
    buf[0:POOL_HALO, :] = jnp.where(first, 0.0, halo_ref[...])
    buf[POOL_HALO:POOL_HALO + ts, :] = u_ref[...]
    pos = (i % tiles_per_seq) * ts + lax.broadcasted_iota(jnp.int32, (ts, 1), 0)
    for g, w in enumerate(POOL_WINDOWS):
        c0, c1 = g * POOL_GROUP, (g + 1) * POOL_GROUP
        cur = buf[POOL_HALO:POOL_HALO + ts, c0:c1]
        acc = cur
        for j in range(1, w):
            acc = acc + buf[POOL_HALO - j:POOL_HALO - j + ts, c0:c1]
        count = jnp.minimum(pos + 1, w).astype(F32)
        d = acc / count - cur
        y = jnp.dot(d.astype(BF16), w_ref[g], preferred_element_type=F32)
        o_ref[:, c0:c1] = (y * scale_ref[:, c0:c1]).astype(o_ref.dtype)


def _pool(u_main, pool_w_bf, pool_scale, seq):
    t = u_main.shape[0]
    ts = min(512, seq)
    tiles_per_seq = seq // ts
    hb = ts // POOL_HALO
    return pl.pallas_call(
        functools.partial(_pool_body, ts=ts, tiles_per_seq=tiles_per_seq),
        out_shape=jax.ShapeDtypeStruct((t, POOL_WIDTH), BF16),
        grid=(t // ts,),
        in_specs=[
            pl.BlockSpec((ts, POOL_WIDTH), lambda i: (i, 0)),
            pl.BlockSpec((POOL_HALO, POOL_WIDTH), lambda i: (jnp.maximum(i * hb - 1, 0), 0)),
            pl.BlockSpec((len(POOL_WINDOWS), POOL_GROUP, POOL_GROUP), lambda i: (0, 0, 0)),
            pl.BlockSpec((1, POOL_WIDTH), lambda i: (0, 0)),
        ],
        out_specs=pl.BlockSpec((ts, POOL_WIDTH), lambda i: (i, 0)),
        scratch_shapes=[pltpu.VMEM((ts + POOL_HALO, POOL_WIDTH), F32)],
        compiler_params=_cparams(1),
        name="pool_mixer",
    )(u_main, u_main, pool_w_bf, pool_scale.reshape(1, POOL_WIDTH))


CONV_HALO = 8
PAIR = 2 * SSM_HEAD_DIM
N_PAIRS = SSM_HEADS // 2
PAIRS_PER_GROUP = N_PAIRS // SSM_GROUPS


def _softplus(x):
    return jnp.maximum(x, 0.0) + jnp.log1p(jnp.exp(-jnp.abs(x)))


def _silu(x):
    return x * jax.nn.sigmoid(x)


def _ssd_body(z_ref, xs_ref, bc_ref, dt_ref, convw_ref, convb_ref, dtb_row_ref, dtb_col_ref,
              alog_row_ref, alog_col_ref, dskip_ref, normw_ref, o_ref,
              buf, xbc, carry, state, ybuf):
    L = SSM_CHUNK
    c = pl.program_id(1)

    @pl.when(c == 0)
    def _():
        carry[...] = jnp.zeros_like(carry)
        state[...] = jnp.zeros_like(state)

    buf[0:CONV_HALO, :] = carry[...]
    buf[CONV_HALO:CONV_HALO + L, 0:SSM_INNER] = xs_ref[...]
    buf[CONV_HALO:CONV_HALO + L, SSM_INNER:SSM_CONV_DIM] = bc_ref[...]
    carry[...] = buf[L:L + CONV_HALO, :]
    cw = 512
    for j in range(SSM_CONV_DIM // cw):
        cs = slice(j * cw, (j + 1) * cw)
        acc = convb_ref[:, cs] + buf[CONV_HALO - 3:CONV_HALO - 3 + L, cs] * convw_ref[0:1, cs]
        for i in range(1, SSM_CONV):
            off = CONV_HALO - 3 + i
            acc = acc + buf[off:off + L, cs] * convw_ref[i:i + 1, cs]
        xbc[:, cs] = _silu(acc)

    ri = lax.broadcasted_iota(jnp.int32, (L, L), 0)
    ci = lax.broadcasted_iota(jnp.int32, (L, L), 1)
    tril = ri >= ci
    dt_blk = dt_ref[...]
    dt = _softplus(dt_blk + dtb_row_ref[...])
    da = dt * (-jnp.exp(alog_row_ref[...]))
    acum = jnp.dot(tril.astype(F32), da, precision=lax.Precision.HIGHEST,
                   preferred_element_type=F32)
    dt_t = _softplus(dt_blk.T + dtb_col_ref[...])
    da_t = dt_t * (-jnp.exp(alog_col_ref[...]))
    acum_t = jnp.dot(da_t, (ri <= ci).astype(F32), precision=lax.Precision.HIGHEST,
                     preferred_element_type=F32)
    w_t = jnp.exp(acum_t[:, L - 1:L] - acum_t) * dt_t
    eac = jnp.exp(acum)
    cd_row = jnp.exp(acum[L - 1:L, :])

    lane = lax.broadcasted_iota(jnp.int32, (L, PAIR), 1)
    lane_row = lax.broadcasted_iota(jnp.int32, (1, PAIR), 1)
    nt = (((1,), (1,)), ((), ()))
    for g in range(SSM_GROUPS):
        b0 = SSM_INNER + g * SSM_STATE
        c0 = SSM_INNER + SSM_GROUPS * SSM_STATE + g * SSM_STATE
        bm = xbc[:, b0:b0 + SSM_STATE]
        cm = xbc[:, c0:c0 + SSM_STATE]
        cb = lax.dot_general(cm.astype(BF16), bm.astype(BF16), nt, preferred_element_type=F32)
        bm_t = bm.T
        for j in range(PAIRS_PER_GROUP):
            p = g * PAIRS_PER_GROUP + j
            xs_pair = xbc[:, p * PAIR:(p + 1) * PAIR].astype(BF16)
            st = state[p]
            lhs_y, lhs_s = [], []
            for h in (2 * p, 2 * p + 1):
                seg = acum[:, h:h + 1] - acum_t[h:h + 1, :]
                dec = jnp.exp(jnp.where(tril, seg, -jnp.inf))
                m_h = cb * dec * dt_t[h:h + 1, :]
                e_h = eac[:, h:h + 1] * cm
                lhs_y.append(jnp.concatenate([m_h, e_h], axis=1).astype(BF16))
                lhs_s.append((bm_t * w_t[h:h + 1, :]).astype(BF16))
            rhs = jnp.concatenate([xs_pair, st.astype(BF16)], axis=0)
            r = jnp.dot(jnp.concatenate(lhs_y, axis=0), rhs, preferred_element_type=F32)
            ybuf[:, p * PAIR:(p + 1) * PAIR] = jnp.where(lane < SSM_HEAD_DIM, r[0:L], r[L:2 * L])
            s_new = jnp.dot(jnp.concatenate(lhs_s, axis=0), xs_pair, preferred_element_type=F32)
            contrib = jnp.where(lane < SSM_HEAD_DIM, s_new[0:L], s_new[L:2 * L])
            cdp = jnp.where(lane_row < SSM_HEAD_DIM, cd_row[:, 2 * p:2 * p + 1], cd_row[:, 2 * p + 1:2 * p + 2])
            state[p] = st * cdp + contrib

    gw = SSM_INNER // SSM_GROUPS
    for g in range(SSM_GROUPS):
        cs = slice(g * gw, (g + 1) * gw)
        y = ybuf[:, cs] + xbc[:, cs] * dskip_ref[:, cs]
        y = y * _silu(z_ref[:, cs])
        ms = jnp.mean(y * y, axis=-1, keepdims=True)
        o_ref[:, cs] = (y * lax.rsqrt(ms + RMS_EPS) * normw_ref[:, cs]).astype(o_ref.dtype)


def _ssd(u_main, dt_raw, conv_w, conv_b, dt_bias, a_log, d_skip, norm_w, batch, seq):
    t = u_main.shape[0]
    L = SSM_CHUNK
    nc = seq // L
    pad = LANES - SSM_HEADS
    dtb = jnp.pad(dt_bias.astype(F32), (0, pad))
    alog = jnp.pad(a_log.astype(F32), (0, pad))
    dskip = jnp.repeat(d_skip.astype(F32), SSM_HEAD_DIM).reshape(1, SSM_INNER)
    zcol = POOL_WIDTH // SSM_INNER
    xcol = (POOL_WIDTH + SSM_INNER) // SSM_INNER
    bc_w = 2 * SSM_GROUPS * SSM_STATE
    bcol = (POOL_WIDTH + 2 * SSM_INNER) // bc_w
    rowmap = lambda col: (lambda b, c: (b * nc + c, col))
    full = lambda shape: pl.BlockSpec(shape, lambda b, c: (0,) * len(shape))
    return pl.pallas_call(
        _ssd_body,
        out_shape=jax.ShapeDtypeStruct((t, SSM_INNER), BF16),
        grid=(batch, nc),
        in_specs=[
            pl.BlockSpec((L, SSM_INNER), rowmap(zcol)),
            pl.BlockSpec((L, SSM_INNER), rowmap(xcol)),
            pl.BlockSpec((L, bc_w), rowmap(bcol)),
            pl.BlockSpec((L, LANES), rowmap(0)),
            full((SSM_CONV, SSM_CONV_DIM)),
            full((1, SSM_CONV_DIM)),
            full((1, LANES)),
            full((LANES, 1)),
            full((1, LANES)),
            full((LANES, 1)),
            full((1, SSM_INNER)),
            full((1, SSM_INNER)),
        ],
        out_specs=pl.BlockSpec((L, SSM_INNER), rowmap(0)),
        scratch_shapes=[
            pltpu.VMEM((L + CONV_HALO, SSM_CONV_DIM), F32),
            pltpu.VMEM((L, SSM_CONV_DIM), F32),
            pltpu.VMEM((CONV_HALO, SSM_CONV_DIM), F32),
            pltpu.VMEM((N_PAIRS, SSM_STATE, PAIR), F32),
            pltpu.VMEM((L, SSM_INNER), F32),
        ],
        compiler_params=_cparams(2),
        name="ssd_mixer",
    )(u_main, u_main, u_main, dt_raw, conv_w.astype(F32), conv_b.reshape(1, SSM_CONV_DIM).astype(F32),
      dtb.reshape(1, LANES), dtb.reshape(LANES, 1), alog.reshape(1, LANES), alog.reshape(LANES, 1),
      dskip, norm_w.reshape(1, SSM_INNER).astype(F32))


def _t5_bucket(dist):
    max_exact = REL_BUCKETS // 2
    d = np.maximum(dist, 1).astype(np.float32)
    large = max_exact + (np.log(d / max_exact) / np.log(REL_MAX_DIST / max_exact)
                         * (REL_BUCKETS - max_exact)).astype(np.int32)
    large = np.minimum(large, REL_BUCKETS - 1)
    return np.where(dist < max_exact, dist, large).astype(np.int32)


def _attn_bias_tables(rel_bias):
    blk = ATTN_BLOCK
    delta = blk + np.arange(blk)[:, None] - np.arange(2 * blk)[None, :]
    tabs = []
    for window, dil in DILATED:
        n_back = window // dil
        assert n_back <= blk
        band = (delta >= 0) & (delta <= n_back)
        bucket = _t5_bucket(np.clip(delta, 0, n_back) * dil)
        rel = jnp.zeros((ATTN_HEADS,) + bucket.shape, F32)
        for bkt in np.unique(bucket[band]):
            rel = jnp.where((bucket == bkt)[None], rel_bias[bkt].astype(F32)[:, None, None], rel)
        tabs.append(jnp.where(band[None], rel, NEG))
    return jnp.stack(tabs, 0)


ATTN_GROUP = 16


def _attn_body(q_ref, kp_ref, kc_ref, vp_ref, vc_ref, bias_ref, o_ref, m_ref, l_ref, acc_ref):
    blk = ATTN_BLOCK
    grp = ATTN_GROUP
    seq_first = pl.program_id(2) == 0
    nt = (((1,), (1,)), ((), ()))
    col = lax.broadcasted_iota(jnp.int32, (blk, 2 * blk), 1)

    def rows(start, n, d):
        if d > 1:
            return pl.ds(start, n, stride=d)
        return pl.ds(start if isinstance(start, int) else pl.multiple_of(start, blk), n)

    for bi, (_, d) in enumerate(DILATED):
        span = blk * d
        nblk = ATTN_SUPER // span
        bias = bias_ref[bi]
        bias_first = jnp.where(jnp.logical_and(seq_first, col < blk), NEG, bias)

        def block(qs, first, bi=bi, d=d, span=span, bias=bias, bias_first=bias_first):
            qr = rows(qs, blk, d)
            q = q_ref[qr, :].astype(BF16)
            if first:
                pr = rows(ATTN_SUPER - span + qs, blk, d)
                k = jnp.concatenate([kp_ref[pr, :], kc_ref[qr, :]], axis=0)
                v = jnp.concatenate([vp_ref[pr, :], vc_ref[qr, :]], axis=0)
                b = bias_first
            else:
                wr = rows(qs - span, 2 * blk, d)
                k = kc_ref[wr, :]
                v = vc_ref[wr, :]
                b = bias
            s = lax.dot_general(q, k.astype(BF16), nt, preferred_element_type=F32) + b
            rm = jnp.max(s, axis=1, keepdims=True)
            p = jnp.exp(s - rm)
            rs = jnp.sum(p, axis=1, keepdims=True)
            pv = jnp.dot(p.astype(BF16), v.astype(BF16), preferred_element_type=F32)
            m_ref[bi, qr, :] = jnp.broadcast_to(rm, (blk, LANES))
            l_ref[bi, qr, :] = jnp.broadcast_to(rs, (blk, LANES))
            acc_ref[bi, qr, :] = pv

        if nblk == 1:
            def body(i, c, block=block):
                for u in range(grp):
                    block(i * grp + u, True)
                return c
            lax.fori_loop(0, d // grp, body, 0)
        elif d == 1:
            block(0, True)
            for u in range(1, grp):
                block(u * span, False)

            def body(i, c, block=block, span=span):
                for u in range(grp):
                    block((i * grp + u) * span, False)
                return c
            lax.fori_loop(1, nblk // grp, body, 0)
        else:
            rpi = max(grp // nblk, 1)

            def body(i, c, block=block, span=span, nblk=nblk, rpi=rpi):
                for u in range(rpi):
                    r = i * rpi + u
                    block(r, True)
                    for n in range(1, nblk):
                        block(n * span + r, False)
                return c
            lax.fori_loop(0, d // rpi, body, 0)

    def merge(c, carry):
        rr = pl.ds(pl.multiple_of(c * blk, blk), blk)
        ms = [m_ref[i, rr, :] for i in range(len(DILATED))]
        mm = functools.reduce(jnp.maximum, ms)
        ws = [jnp.exp(m - mm) for m in ms]
        den = functools.reduce(lambda a, b: a + b, [w * l_ref[i, rr, :] for i, w in enumerate(ws)])
        num = functools.reduce(lambda a, b: a + b, [w * acc_ref[i, rr, :] for i, w in enumerate(ws)])
        o_ref[rr, :] = (num / den).astype(o_ref.dtype)
        return carry

    lax.fori_loop(0, ATTN_SUPER // blk, merge, 0)


def _attention(qkv_heads, bias_tabs, batch, seq):
    t = qkv_heads.shape[1]
    hn = ATTN_HEADS
    sup = ATTN_SUPER
    nst = seq // sup
    assert seq % sup == 0
    cur = lambda off: (lambda b, h, s: (off + h, b * nst + s, 0))
    prev = lambda off: (lambda b, h, s: (off + h, b * nst + jnp.maximum(s - 1, 0), 0))
    slab = lambda imap: pl.BlockSpec((None, sup, LANES), imap)
    return pl.pallas_call(
        _attn_body,
        out_shape=jax.ShapeDtypeStruct((t, D_MODEL), BF16),
        grid=(batch, hn, nst),
        in_specs=[
            slab(cur(0)),
            slab(prev(hn)), slab(cur(hn)),
            slab(prev(2 * hn)), slab(cur(2 * hn)),
            pl.BlockSpec((len(DILATED), None, ATTN_BLOCK, 2 * ATTN_BLOCK), lambda b, h, s: (0, h, 0, 0)),
        ],
        out_specs=pl.BlockSpec((sup, LANES), lambda b, h, s: (b * nst + s, h)),
        scratch_shapes=[pltpu.VMEM((len(DILATED), sup, LANES), F32)] * 3,
        compiler_params=_cparams(3),
        name="dilated_attention",
    )(qkv_heads, qkv_heads, qkv_heads, qkv_heads, qkv_heads, bias_tabs)


def _router_body(h_ref, w_ref, b_ref, idx_ref, gate_ref):
    lo, hi = _unpack_pairs(h_ref[...])
    logits = (jnp.dot(lo.astype(BF16), w_ref[:HALF, :], preferred_element_type=F32)
              + jnp.dot(hi.astype(BF16), w_ref[HALF:, :], preferred_element_type=F32) + b_ref[...])
    lane = lax.broadcasted_iota(jnp.int32, logits.shape, 1)
    vals, idxs = [], []
    for _ in range(TOP_K):
        m = jnp.max(logits, axis=1, keepdims=True)
        idx = jnp.min(jnp.where(logits == m, lane, LANES), axis=1, keepdims=True)
        vals.append(m)
        idxs.append(idx)
        logits = jnp.where(lane == idx, -jnp.inf, logits)
    es = [jnp.exp(v - vals[0]) for v in vals]
    den = es[0]
    for e in es[1:]:
        den = den + e
    idx_out = jnp.zeros(logits.shape, jnp.int32)
    gate_out = jnp.zeros(logits.shape, F32)
    for k in range(TOP_K):
        idx_out = jnp.where(lane == k, idxs[k], idx_out)
        gate_out = jnp.where(lane == k, es[k] / den, gate_out)
    idx_ref[...] = idx_out
    gate_ref[...] = gate_out


def _router(h_pk, router_w, router_b):
    t, hd = h_pk.shape
    d = 2 * hd
    tm = min(1024, t)
    pad = LANES - N_EXPERTS
    w = jnp.pad(router_w, ((0, 0), (0, pad))).astype(BF16)
    b = jnp.pad(router_b.astype(F32), (0, pad), constant_values=NEG).reshape(1, LANES)
    idx, gates = pl.pallas_call(
        _router_body,
        out_shape=(jax.ShapeDtypeStruct((t, LANES), jnp.int32), jax.ShapeDtypeStruct((t, LANES), F32)),
        grid=(t // tm,),
        in_specs=[pl.BlockSpec((tm, hd), lambda i: (i, 0)),
                  pl.BlockSpec((d, LANES), lambda i: (0, 0)),
                  pl.BlockSpec((1, LANES), lambda i: (0, 0))],
        out_specs=(pl.BlockSpec((tm, LANES), lambda i: (i, 0)), pl.BlockSpec((tm, LANES), lambda i: (i, 0))),
        compiler_params=_cparams(1),
        name="moe_router",
    )(h_pk, w, b)
    return idx, gates


def _rank_body(idx_ref, rank_ref, counts_ref, carry):
    @pl.when(pl.program_id(0) == 0)
    def _():
        carry[...] = jnp.zeros_like(carry)

    idx = idx_ref[...]
    tt = idx.shape[0]
    lane = lax.broadcasted_iota(jnp.int32, idx.shape, 1)
    hits = [idx[:, k:k + 1] == lane for k in range(TOP_K)]
    per_expert = functools.reduce(lambda a, b: a + b, [h.astype(F32) for h in hits])
    earlier = (lax.broadcasted_iota(jnp.int32, (tt, tt), 0) > lax.broadcasted_iota(jnp.int32, (tt, tt), 1))
    before = jnp.dot(earlier.astype(BF16), per_expert.astype(BF16), preferred_element_type=F32) + carry[...]
    out = jnp.zeros(idx.shape, jnp.int32)
    for k in range(TOP_K):
        rk = jnp.sum(jnp.where(hits[k], before, 0.0), axis=1, keepdims=True).astype(jnp.int32)
        out = jnp.where(lane == k, rk, out)
    rank_ref[...] = out
    carry[...] = carry[...] + jnp.sum(per_expert, axis=0, keepdims=True)
    counts_ref[...] = carry[...].astype(jnp.int32)


def _slot_body(idx_ref, rank_ref, starts_ref, slot_ref):
    idx = idx_ref[...]
    lane = lax.broadcasted_iota(jnp.int32, idx.shape, 1)
    starts = starts_ref[...].astype(F32)
    out = jnp.zeros(idx.shape, jnp.int32)
    for k in range(TOP_K):
        st = jnp.sum(jnp.where(idx[:, k:k + 1] == lane, starts, 0.0), axis=1, keepdims=True)
        out = jnp.where(lane == k, st.astype(jnp.int32), out)
    slot_ref[...] = out + rank_ref[...]


def _route_plan(top_idx, tm):
    t = top_idx.shape[0]
    a = t * TOP_K
    tt = min(512, t)
    blk = pl.BlockSpec((tt, LANES), lambda i: (i, 0))
    vec = pl.BlockSpec((1, LANES), lambda i: (0, 0))
    rank, counts = pl.pallas_call(
        _rank_body,
        out_shape=(jax.ShapeDtypeStruct((t, LANES), jnp.int32), jax.ShapeDtypeStruct((1, LANES), jnp.int32)),
        grid=(t // tt,),
        in_specs=[blk],
        out_specs=(blk, vec),
        scratch_shapes=[pltpu.VMEM((1, LANES), F32)],
        compiler_params=_cparams(1),
        name="moe_rank",
    )(top_idx)
    counts = counts[0, :N_EXPERTS]
    padded = ((counts + tm - 1) // tm) * tm
    ends = jnp.cumsum(padded)
    starts = ends - padded
    slot = pl.pallas_call(
        _slot_body,
        out_shape=jax.ShapeDtypeStruct((t, LANES), jnp.int32),
        grid=(t // tt,),
        in_specs=[blk, blk, vec],
        out_specs=blk,
        compiler_params=_cparams(1),
        name="moe_slot",
    )(top_idx, rank, jnp.pad(starts, (0, LANES - N_EXPERTS)).reshape(1, LANES).astype(jnp.int32))
    slot = slot[:, :TOP_K].reshape(a)
    p = a + N_EXPERTS * tm
    token = jnp.arange(a, dtype=jnp.int32) // TOP_K
    row_src = jnp.zeros((p,), jnp.int32).at[slot].set(token, unique_indices=True)
    tile_start = jnp.arange(p // tm, dtype=jnp.int32) * tm
    tile_expert = jnp.sum((ends[None, :] <= tile_start[:, None]).astype(jnp.int32), axis=1)
    tile_expert = jnp.minimum(tile_expert, N_EXPERTS - 1).astype(jnp.int32)
    n_used = (ends[-1] // tm).astype(jnp.int32).reshape(1)
    return slot, row_src, tile_expert, n_used


def _deinterleave_body(w_ref, perm_ref, o_ref):
    o_ref[...] = jnp.dot(w_ref[...].astype(BF16), perm_ref[...], preferred_element_type=F32).astype(o_ref.dtype)


def _gate_up_weights(w_gate_up_all, layer):
    _, ne, d, n = w_gate_up_all.shape
    tk = 1024
    src = np.concatenate([np.arange(0, n, 2), np.arange(1, n, 2)])
    perm = np.zeros((n, n), np.float32)
    perm[src, np.arange(n)] = 1.0
    return pl.pallas_call(
        _deinterleave_body,
        out_shape=jax.ShapeDtypeStruct((ne, d, n), BF16),
        grid=(ne, d // tk),
        in_specs=[pl.BlockSpec((None, None, tk, n), lambda e, k: (layer, e, k, 0)),
                  pl.BlockSpec((n, n), lambda e, k: (0, 0))],
        out_specs=pl.BlockSpec((None, tk, n), lambda e, k: (e, k, 0)),
        compiler_params=_cparams(2),
        name="moe_gate_up_prep",
    )(w_gate_up_all, jnp.asarray(perm, BF16))


SUBLANES = 8
GATHER_BUFS = 3


def _start_row_gathers(idx_of, n_rows, src3_ref, dst_of, sem):
    def group(gi, c):
        for u in range(SUBLANES):
            row = idx_of(gi * SUBLANES + u)
            pltpu.make_async_copy(src3_ref.at[row >> 3, pl.ds(row & 7, 1)], dst_of(gi, u), sem).start()
        return c

    lax.fori_loop(0, n_rows // SUBLANES, group, 0)


def _expert_body(te_ref, nused_ref, src_ref, h_ref, wgu_ref, bgu_ref, wd_ref, bd_ref, o_ref,
                 xbuf, sem, *, tm):
    i = pl.program_id(0)
    n_used = nused_ref[0]

    def gather(tile, buf):
        base = tile * tm
        _start_row_gathers(lambda r: src_ref[base + r], tm, h_ref,
                           lambda gi, u: xbuf.at[buf, gi, pl.ds(u, 1)], sem.at[buf])

    def gather_inline(tile, buf, r_lo, r_hi):
        base = tile * tm
        for r in range(r_lo, r_hi):
            row = src_ref[base + r]
            pltpu.make_async_copy(h_ref.at[row >> 3, pl.ds(row & 7, 1)],
                                  xbuf.at[buf, r // SUBLANES, pl.ds(r % SUBLANES, 1)], sem.at[buf]).start()

    def wait_tile(buf):
        pltpu.make_async_copy(h_ref.at[pl.ds(0, tm // SUBLANES)], xbuf.at[buf], sem.at[buf]).wait()

    last_tile = pl.num_programs(0) - 1

    @pl.when(i == 0)
    def _():
        gather(0, 0)
        gather(1, 1)

    @pl.when(i < n_used)
    def _():
        buf = i % GATHER_BUFS
        wait_tile(buf)
        lo, hi = _unpack_pairs(xbuf[buf].reshape(tm, HALF))
        hdn = (jnp.dot(lo.astype(BF16), wgu_ref[:HALF, :], preferred_element_type=F32)
               + jnp.dot(hi.astype(BF16), wgu_ref[HALF:, :], preferred_element_type=F32) + bgu_ref[...])
        gate = jnp.minimum(hdn[:, :D_EXPERT], SWIGLU_LIMIT)
        up = jnp.clip(hdn[:, D_EXPERT:], -SWIGLU_LIMIT, SWIGLU_LIMIT)
        act = (up + 1.0) * gate * jax.nn.sigmoid(SWIGLU_ALPHA * gate)
        nxt = jnp.minimum(i + 2, last_tile)
        nbuf = (i + 2) % GATHER_BUFS
        gather_inline(nxt, nbuf, 0, tm // 2)
        y = jnp.dot(act.astype(BF16), wd_ref[...], preferred_element_type=F32) + bd_ref[...]
        o_ref[...] = _pack_pairs(y)
        gather_inline(nxt, nbuf, tm // 2, tm)

    @pl.when(i == n_used)
    def _():
        wait_tile(i % GATHER_BUFS)
        wait_tile((i + 1) % GATHER_BUFS)

    @pl.when(i >= n_used)
    def _():
        o_ref[...] = jnp.zeros_like(o_ref)


def _experts(h_pk, row_src, tile_expert, n_used, wgu, bgu, wd, bd, tm):
    p = row_src.shape[0]
    t, hd = h_pk.shape
    d = 2 * hd
    return pl.pallas_call(
        functools.partial(_expert_body, tm=tm),
        out_shape=jax.ShapeDtypeStruct((p, hd), jnp.uint32),
        grid_spec=pltpu.PrefetchScalarGridSpec(
            num_scalar_prefetch=3,
            grid=(p // tm,),
            in_specs=[
                pl.BlockSpec(memory_space=pl.ANY),
                pl.BlockSpec((None, d, 2 * D_EXPERT), lambda i, te, nu, src: (te[i], 0, 0)),
                pl.BlockSpec((None, 1, 2 * D_EXPERT), lambda i, te, nu, src: (te[i], 0, 0)),
                pl.BlockSpec((None, D_EXPERT, d), lambda i, te, nu, src: (te[i], 0, 0)),
                pl.BlockSpec((None, 1, d), lambda i, te, nu, src: (te[i], 0, 0)),
            ],
            out_specs=pl.BlockSpec((tm, hd), lambda i, te, nu, src: (i, 0)),
            scratch_shapes=[pltpu.VMEM((GATHER_BUFS, tm // SUBLANES, SUBLANES, hd), jnp.uint32),
                            pltpu.SemaphoreType.DMA((GATHER_BUFS,))],
        ),
        compiler_params=_cparams(1),
        name="moe_experts",
    )(tile_expert, n_used, row_src, h_pk.reshape(t // SUBLANES, SUBLANES, hd), wgu, bgu, wd, bd)


COMBINE_GROUP = 32


def _combine_body(slot_ref, y_ref, gate_ref, h_ref, g_ref, b_ref, o_ref, obf_ref, ybuf, sem, *, tc):
    i = pl.program_id(0)
    n = pl.num_programs(0)

    def gather(tile, buf):
        base = tile * tc
        for k in range(TOP_K):
            _start_row_gathers(lambda r, k=k: slot_ref[(base + r) * TOP_K + k], tc, y_ref,
                               lambda gi, u, k=k: ybuf.at[buf, k, gi, pl.ds(u, 1)], sem.at[buf])

    def wait_tile(b):
        for k in range(TOP_K):
            pltpu.make_async_copy(y_ref.at[pl.ds(0, tc // SUBLANES)], ybuf.at[b, k], sem.at[b]).wait()

    @pl.when(i == 0)
    def _():
        gather(0, 0)
        gather(jnp.minimum(1, n - 1), 1)

    buf = i % GATHER_BUFS
    nbuf = (i + 2) % GATHER_BUFS
    wait_tile(buf)

    nxt_base = jnp.minimum(i + 2, n - 1) * tc
    grp = COMBINE_GROUP

    def group(gi, c):
        r0 = pl.multiple_of(gi * grp, grp)
        rows = pl.ds(r0, grp)
        srows = pl.ds(pl.multiple_of(gi * (grp // SUBLANES), grp // SUBLANES), grp // SUBLANES)
        gates = gate_ref[rows, :]
        ffn_lo = ffn_hi = None
        for k in range(TOP_K):
            lo, hi = _unpack_pairs(ybuf[buf, k, srows].reshape(grp, HALF))
            gk = gates[:, k:k + 1]
            ffn_lo = gk * lo if ffn_lo is None else ffn_lo + gk * lo
            ffn_hi = gk * hi if ffn_hi is None else ffn_hi + gk * hi
        ffn = jnp.concatenate([ffn_lo, ffn_hi], axis=1)
        y = _layer_norm_rows(DEEPNORM_ALPHA * h_ref[rows, :] + ffn, g_ref[...], b_ref[...])
        o_ref[rows, :] = y
        obf_ref[rows, :] = y.astype(BF16)
        for gg in range(grp // SUBLANES):
            for u in range(SUBLANES):
                for k in range(TOP_K):
                    s = slot_ref[(nxt_base + r0 + gg * SUBLANES + u) * TOP_K + k]
                    pltpu.make_async_copy(y_ref.at[s >> 3, pl.ds(s & 7, 1)],
                                          ybuf.at[nbuf, k, gi * (grp // SUBLANES) + gg, pl.ds(u, 1)],
                                          sem.at[nbuf]).start()
        return c

    lax.fori_loop(0, tc // grp, group, 0)

    @pl.when(i == n - 1)
    def _():
        wait_tile((i + 1) % GATHER_BUFS)
        wait_tile((i + 2) % GATHER_BUFS)


def _combine_ln(slot, y_pk, gates, h, g, b):
    t, d = h.shape
    p, hd = y_pk.shape
    tc = min(128, t)
    row = lambda i, s: (i, 0)
    vec = lambda i, s: (0, 0)
    return pl.pallas_call(
        functools.partial(_combine_body, tc=tc),
        out_shape=(jax.ShapeDtypeStruct((t, d), F32), jax.ShapeDtypeStruct((t, d), BF16)),
        grid_spec=pltpu.PrefetchScalarGridSpec(
            num_scalar_prefetch=1,
            grid=(t // tc,),
            in_specs=[
                pl.BlockSpec(memory_space=pl.ANY),
                pl.BlockSpec((tc, LANES), row),
                pl.BlockSpec((tc, d), row),
                pl.BlockSpec((1, d), vec),
                pl.BlockSpec((1, d), vec),
            ],
            out_specs=(pl.BlockSpec((tc, d), row), pl.BlockSpec((tc, d), row)),
            scratch_shapes=[pltpu.VMEM((GATHER_BUFS, TOP_K, tc // SUBLANES, SUBLANES, hd), jnp.uint32),
                            pltpu.SemaphoreType.DMA((GATHER_BUFS,))],
        ),
        compiler_params=_cparams(1),
        name="moe_combine_ln",
    )(slot, y_pk.reshape(p // SUBLANES, SUBLANES, hd), gates, h, g.reshape(1, d), b.reshape(1, d))


def _moe_ln(h, h_pk, layer, router_w, router_b, w_gate_up_all, b_gate_up, w_down_all, b_down, g, b):
    tm = MOE_TILE
    top_idx, gates = _router(h_pk, router_w, router_b)
    slot, row_src, tile_expert, n_used = _route_plan(top_idx, tm)
    wgu = _gate_up_weights(w_gate_up_all, layer)
    bgu = jnp.concatenate([b_gate_up[..., 0::2], b_gate_up[..., 1::2]], axis=-1).astype(F32)
    bgu = bgu.reshape(N_EXPERTS, 1, 2 * D_EXPERT)
    wd = _layer_bf16(w_down_all, layer, "cast_w_down").reshape(N_EXPERTS, D_EXPERT, D_MODEL)
    bd = b_down.astype(F32).reshape(N_EXPERTS, 1, D_MODEL)
    y_pk = _experts(h_pk, row_src, tile_expert, n_used, wgu, bgu, wd, bd, tm)
    return _combine_ln(slot, y_pk, gates, h, g, b)


def _even_mixer(h_bf, i, w_in_all, pool_w, pool_scale, conv_w, conv_b, dt_bias, a_log, d_skip, norm_w,
                w_out_all, batch, seq):
    w_in_bf = _layer_bf16(w_in_all, i, "cast_even_w_in")
    w_dt = jnp.pad(w_in_all[i, :, EVEN_MAIN:], ((0, 0), (0, LANES - SSM_HEADS))).astype(BF16)
    u_main = _matmul([h_bf], w_in_bf, EVEN_MAIN, 512, F32, "even_in_proj")
    dt_raw = _matmul([h_bf], w_dt, LANES, LANES, F32, "even_dt_proj")
    y_pool = _pool(u_main, pool_w.astype(BF16), pool_scale.astype(F32), seq)
    y_ssd = _ssd(u_main, dt_raw, conv_w, conv_b, dt_bias, a_log, d_skip, norm_w, batch, seq)
    w_out_bf = _layer_bf16(w_out_all, i, "cast_even_w_out")
    return _matmul([y_pool, y_ssd], w_out_bf, D_MODEL, 512, BF16, "even_out_proj")


def _attn_mixer(h_bf, i, w_qkv_all, w_out_all, bias_tabs, batch, seq):
    w_qkv_bf = _layer_bf16(w_qkv_all, i, "cast_attn_w_qkv")
    qkv_heads = _matmul([h_bf], w_qkv_bf, 3 * D_MODEL, 512, F32, "attn_qkv_proj", head_out=True,
                        scaled_cols=D_MODEL, scale=ATTN_HEAD_DIM ** -0.5)
    o = _attention(qkv_heads, bias_tabs, batch, seq)
    w_out_bf = _layer_bf16(w_out_all, i, "cast_attn_w_out")
    return _matmul([o], w_out_bf, D_MODEL, 512, BF16, "attn_out_proj")


def kernel(x, rel_bias, even_w_in, pool_w, pool_scale, conv_w, conv_b, dt_bias, a_log, d_skip, ssm_norm_w, even_w_out, attn_w_qkv, attn_w_out, ln1_g, ln1_b, ln2_g, ln2_b, router_w, router_b, w_gate_up, b_gate_up, w_down, b_down):
    batch, seq, d = x.shape
    t = batch * seq
    h = x.reshape(t, d).astype(F32)
    h_bf = _cast_rows_bf16(h, 0, t, "cast_x")
    bias_tabs = _attn_bias_tables(rel_bias)
    for layer in range(DEPTH):
        i = layer // 2
        if layer % 2 == 0:
            mix = _even_mixer(h_bf, i, even_w_in, pool_w[i], pool_scale[i], conv_w[i], conv_b[i],
                              dt_bias[i], a_log[i], d_skip[i], ssm_norm_w[i], even_w_out, batch, seq)
        else:
            mix = _attn_mixer(h_bf, i, attn_w_qkv, attn_w_out, bias_tabs, batch, seq)
        h, h_pk = _residual_ln(h, mix, ln1_g[layer], ln1_b[layer])
        h, h_bf = _moe_ln(h, h_pk, layer, router_w[layer], router_b[layer], w_gate_up, b_gate_up[layer],
                          w_down, b_down[layer], ln2_g[layer], ln2_b[layer])
    return h.reshape(batch, seq, d).astype(x.dtype)
```
